```python
import math
import jax
import jax.numpy as jnp
from jax import lax
import numpy as np

D_MODEL = 1024
BATCH = 32
SEQ = 2048
DEPTH = 4

GRID_W = 64
CTX_LEN = 256
EPS = 1e-6
ROPE_THETA = 10000.0
N_BRANCHES = 4
BRANCH_WIDTH = D_MODEL // 2
D_FF = 4 * D_MODEL

SSM_HEAD_DIM = 64
SSM_HEADS = BRANCH_WIDTH // SSM_HEAD_DIM
SSM_D_INNER = SSM_HEADS * SSM_HEAD_DIM
SSM_GROUPS = 2
SSM_HPG = SSM_HEADS // SSM_GROUPS
SSM_STATE = 128
SSM_CONV = 5
SSM_CHUNK = 128
SSM_XBC = SSM_D_INNER + 2 * SSM_GROUPS * SSM_STATE

SC_WIDTH = BRANCH_WIDTH
SC_CONV = 3

MLA_HEADS = 8
MLA_NOPE = 64
MLA_ROPE = 32
MLA_V = BRANCH_WIDTH // MLA_HEADS
MLA_Q_RANK = 384
MLA_KV_RANK = 256
ATTN_BLOCK = 128

FNET_GROUPS = 4
FNET_GROUP_DIM = BRANCH_WIDTH // FNET_GROUPS

IN_SIZES = (SSM_D_INNER, SSM_XBC, 2 * SSM_HEADS, 3 * SC_WIDTH, MLA_Q_RANK, MLA_KV_RANK, MLA_ROPE, BRANCH_WIDTH, N_BRANCHES * D_MODEL)
IN_SPLITS = tuple(int(v) for v in np.cumsum(IN_SIZES)[:-1])
D_IN = int(sum(IN_SIZES))

kernel_name = 'hybrid_ssd_conv_mla_fourier_dit_block'


def rmsnorm(x, g):
    xf = x.astype(jnp.float32)
    y = xf * lax.rsqrt(jnp.mean(xf * xf, axis=-1, keepdims=True) + EPS)
    return (y * g.astype(jnp.float32)).astype(x.dtype)


def modulate(h, shift, scale):
    return h * (1 + scale) + shift


def dwconv(u, w):
    pad = (w.shape[0] - 1) // 2
    return lax.conv_general_dilated(u, w[:, None, :].astype(u.dtype), window_strides=(1,), padding=[(pad, pad)], dimension_numbers=('NWC', 'WIO', 'NWC'), feature_group_count=u.shape[-1])


def axial_rope(n_tokens):
    rows = n_tokens // GRID_W
    row = jnp.repeat(jnp.arange(rows, dtype=jnp.float32), GRID_W)
    col = jnp.tile(jnp.arange(GRID_W, dtype=jnp.float32), rows)
    n_freq = MLA_ROPE // 4
    inv = ROPE_THETA ** (-jnp.arange(n_freq, dtype=jnp.float32) / n_freq)
    ang = jnp.stack([row[:, None] * inv, col[:, None] * inv], axis=1)
    return jnp.cos(ang), jnp.sin(ang)


def rope2d(x, cos, sin):
    xs = x.astype(jnp.float32).reshape(x.shape[:-1] + (2, 2, MLA_ROPE // 4))
    x1, x2 = xs[..., 0, :], xs[..., 1, :]
    out = jnp.stack([x1 * cos - x2 * sin, x1 * sin + x2 * cos], axis=-2)
    return out.reshape(x.shape).astype(x.dtype)


def ssd_scan(x, dt, a, b, c, h0, want_y):
    f32 = jnp.float32
    bsz, n = x.shape[:2]
    nc, q = n // SSM_CHUNK, SSM_CHUNK
    xdt = (x.astype(f32) * dt[..., None]).reshape(bsz, nc, q, SSM_GROUPS, SSM_HPG, SSM_HEAD_DIM)
    bm = b.astype(f32).reshape(bsz, nc, q, SSM_GROUPS, SSM_STATE)
    cm = c.astype(f32).reshape(bsz, nc, q, SSM_GROUPS, SSM_STATE)
    la = (dt * a).reshape(bsz, nc, q, SSM_GROUPS, SSM_HPG)
    cum = jnp.cumsum(la, axis=2)
    states = jnp.einsum('bcjgn,bcjgr,bcjgrp->bcgrpn', bm, jnp.exp(cum[:, :, -1:] - cum), xdt)
    chunk_decay = jnp.exp(cum[:, :, -1])

    def step(h, inp):
        s, d = inp
        return h * d[..., None, None] + s, h

    h_final, h_prev = lax.scan(step, h0.reshape(bsz, SSM_GROUPS, SSM_HPG, SSM_HEAD_DIM, SSM_STATE), (jnp.moveaxis(states, 1, 0), jnp.moveaxis(chunk_decay, 1, 0)))
    h_final = h_final.reshape(bsz, SSM_HEADS, SSM_HEAD_DIM, SSM_STATE)
    if not want_y:
        return None, h_final
    h_prev = jnp.moveaxis(h_prev, 0, 1)
    lower = jnp.tril(jnp.ones((q, q), dtype=bool))[:, :, None, None]
    seg = jnp.exp(jnp.where(lower, cum[:, :, :, None] - cum[:, :, None, :], -jnp.inf))
    cb = jnp.einsum('bcign,bcjgn->bcijg', cm, bm)
    y_diag = jnp.einsum('bcijg,bcijgr,bcjgrp->bcigrp', cb, seg, xdt)
    y_off = jnp.einsum('bcign,bcigr,bcgrpn->bcigrp', cm, jnp.exp(cum), h_prev)
    return (y_diag + y_off).reshape(bsz, n, SSM_HEADS, SSM_HEAD_DIM), h_final


def ssd_stream(z, xbc, dt_raw, h_init, conv_w, conv_b, a_log, dt_bias, d_skip, norm_g, want_y):
    f32 = jnp.float32
    bsz, n = z.shape[:2]
    xbc = jax.nn.silu(dwconv(xbc, conv_w) + conv_b.astype(xbc.dtype))
    xs, bmat, cmat = jnp.split(xbc, [SSM_D_INNER, SSM_D_INNER + SSM_GROUPS * SSM_STATE], axis=-1)
    xs = xs.reshape(bsz, n, SSM_HEADS, SSM_HEAD_DIM)
    bmat = bmat.reshape(bsz, n, SSM_GROUPS, SSM_STATE)
    cmat = cmat.reshape(bsz, n, SSM_GROUPS, SSM_STATE)
    a = -jnp.exp(a_log.astype(f32))
    dt = jax.nn.softplus(dt_raw.astype(f32).reshape(bsz, n, 2, SSM_HEADS) + dt_bias.astype(f32))
    flip = lambda t: jnp.flip(t, axis=1)
    y_f, h_f = ssd_scan(xs, dt[:, :, 0], a[0], bmat, cmat, h_init[0], want_y)
    y_b, h_b = ssd_scan(flip(xs), flip(dt[:, :, 1]), a[1], flip(bmat), flip(cmat), h_init[1], want_y)
    if not want_y:
        return None, (h_f, h_b)
    y = y_f + flip(y_b) + d_skip.astype(f32)[:, None] * xs.astype(f32)
    y = y.reshape(bsz, n, SSM_D_INNER) * jax.nn.silu(z.astype(f32))
    y = rmsnorm(y.reshape(bsz, n, SSM_GROUPS, SSM_D_INNER // SSM_GROUPS), norm_g.reshape(SSM_GROUPS, -1))
    return y.reshape(bsz, n, SSM_D_INNER).astype(z.dtype), (h_f, h_b)


def short_conv(u, conv_w):
    bg, cg, xin = jnp.split(u, 3, axis=-1)
    return bg * dwconv(cg * xin, conv_w)


def mla_q(q_a, qa_g, w_qb, cos, sin):
    bsz, n = q_a.shape[:2]
    q = (rmsnorm(q_a, qa_g) @ w_qb).reshape(bsz, n, MLA_HEADS, MLA_NOPE + MLA_ROPE)
    q_nope, q_rope = q[..., :MLA_NOPE], q[..., MLA_NOPE:]
    if cos is not None:
        q_rope = rope2d(q_rope, cos[:, None], sin[:, None])
    return q_nope, q_rope


def mla_kv(kv_a, k_r, kva_g, w_kvb, cos, sin):
    bsz, n = kv_a.shape[:2]
    kv = (rmsnorm(kv_a, kva_g) @ w_kvb).reshape(bsz, n, MLA_HEADS, MLA_NOPE + MLA_V)
    k_rope = k_r if cos is None else rope2d(k_r, cos, sin)
    return kv[..., :MLA_NOPE], k_rope, kv[..., MLA_NOPE:]


def block_attention(q_nope, q_rope, k_nope, k_rope, v):
    bsz, lq = q_nope.shape[:2]
    nb = lq // ATTN_BLOCK
    scale = (MLA_NOPE + MLA_ROPE) ** -0.5

    def one_block(qs):
        qn, qr = qs
        s = jnp.einsum('bqhd,bkhd->bhqk', qn, k_nope, preferred_element_type=jnp.float32) + jnp.einsum('bqhr,bkr->bhqk', qr, k_rope, preferred_element_type=jnp.float32)
        p = jax.nn.softmax(s * scale, axis=-1)
        return jnp.einsum('bhqk,bkhd->bqhd', p.astype(v.dtype), v)

    blocks = lambda t: jnp.moveaxis(t.reshape((bsz, nb, ATTN_BLOCK) + t.shape[2:]), 1, 0)
    o = lax.map(one_block, (blocks(q_nope), blocks(q_rope)))
    return jnp.moveaxis(o, 0, 1).reshape(bsz, lq, MLA_HEADS * MLA_V)


def fourier_mix(u):
    bsz, n = u.shape[:2]
    g = u.astype(jnp.float32).reshape(bsz, n, FNET_GROUPS, FNET_GROUP_DIM)
    return jnp.fft.fft2(g, axes=(1, 3), norm='ortho').real.reshape(bsz, n, BRANCH_WIDTH).astype(u.dtype)


def merge_branches(branches, gate_logits, w_branch, w_out):
    bsz, n = gate_logits.shape[:2]
    gates = jax.nn.sigmoid(gate_logits.astype(jnp.float32)).astype(gate_logits.dtype).reshape(bsz, n, N_BRANCHES, D_MODEL)
    merged = gates[:, :, 0] * (branches[0] @ w_branch[0])
    for k in range(1, N_BRANCHES):
        merged = merged + gates[:, :, k] * (branches[k] @ w_branch[k])
    return merged @ w_out


def hybrid_mixer(h_ctx, h_lat, cos, sin, w_in, ssm_conv_w, ssm_conv_b, ssm_a_log, ssm_dt_bias, ssm_d, ssm_norm_g, sc_conv_w, mla_qa_g, mla_wqb, mla_kva_g, mla_wkvb, w_branch, w_out, need_ctx):
    zc, xbcc, dtc, scc, qac, kvac, krc, fnc, gtc = jnp.split(h_ctx @ w_in, IN_SPLITS, axis=-1)
    zl, xbcl, dtl, scl, qal, kval, krl, fnl, gtl = jnp.split(h_lat @ w_in, IN_SPLITS, axis=-1)
    h0 = jnp.zeros((h_ctx.shape[0], SSM_HEADS, SSM_HEAD_DIM, SSM_STATE), jnp.float32)
    y_ssm_c, ctx_states = ssd_stream(zc, xbcc, dtc, (h0, h0), ssm_conv_w, ssm_conv_b, ssm_a_log, ssm_dt_bias, ssm_d, ssm_norm_g, need_ctx)
    y_ssm_l, _ = ssd_stream(zl, xbcl, dtl, ctx_states, ssm_conv_w, ssm_conv_b, ssm_a_log, ssm_dt_bias, ssm_d, ssm_norm_g, True)
    kn_c, kr_c, v_c = mla_kv(kvac, krc, mla_kva_g, mla_wkvb, None, None)
    kn_l, kr_l, v_l = mla_kv(kval, krl, mla_kva_g, mla_wkvb, cos, sin)
    qn_l, qr_l = mla_q(qal, mla_qa_g, mla_wqb, cos, sin)
    att_l = block_attention(qn_l, qr_l, jnp.concatenate([kn_c, kn_l], axis=1), jnp.concatenate([kr_c, kr_l], axis=1), jnp.concatenate([v_c, v_l], axis=1))
    out_l = merge_branches((y_ssm_l, short_conv(scl, sc_conv_w), att_l, fourier_mix(fnl)), gtl, w_branch, w_out)
    if not need_ctx:
        return None, out_l
    qn_c, qr_c = mla_q(qac, mla_qa_g, mla_wqb, None, None)
    att_c = block_attention(qn_c, qr_c, kn_c, kr_c, v_c)
    out_c = merge_branches((y_ssm_c, short_conv(scc, sc_conv_w), att_c, fourier_mix(fnc)), gtc, w_branch, w_out)
    return out_c, out_l


def sq_relu_mlp(h, w1, w2):
    return jnp.square(jax.nn.relu(h @ w1)) @ w2


def setup_inputs(seed: int = 0) -> dict:
    key = jax.random.key(seed)
    ks = jax.random.split(key, 32)
    f32 = jnp.float32

    def nrm(k, shape, std):
        return jax.random.normal(k, shape, f32) * std

    def gain(k, shape):
        return 1.0 + 0.02 * jax.random.normal(k, shape, f32)

    dt0 = jnp.exp(jax.random.uniform(ks[10], (DEPTH, 2, SSM_HEADS), f32, math.log(1e-3), math.log(1e-1)))
    return {
        'x': nrm(ks[0], (BATCH, SEQ, D_MODEL), 1.0),
        'c': nrm(ks[1], (BATCH, D_MODEL), 1.0),
        'ctx': nrm(ks[2], (BATCH, CTX_LEN, D_MODEL), 1.0),
        'c_ctx': nrm(ks[3], (D_MODEL,), 1.0),
        'ada_w': nrm(ks[4], (DEPTH, D_MODEL, 6 * D_MODEL), 0.25 * D_MODEL ** -0.5),
        'ada_b': nrm(ks[5], (DEPTH, 6 * D_MODEL), 0.01),
        'norm1_g': gain(ks[6], (DEPTH, D_MODEL)),
        'norm2_g': gain(ks[7], (DEPTH, D_MODEL)),
        'w_in': nrm(ks[8], (DEPTH, D_MODEL, D_IN), D_MODEL ** -0.5),
        'ssm_conv_w': nrm(ks[9], (DEPTH, SSM_CONV, SSM_XBC), SSM_CONV ** -0.5),
        'ssm_conv_b': nrm(ks[11], (DEPTH, SSM_XBC), 0.01),
        'ssm_a_log': jnp.log(jax.random.uniform(ks[12], (DEPTH, 2, SSM_HEADS), f32, 1.0, 16.0)),
        'ssm_dt_bias': dt0 + jnp.log(-jnp.expm1(-dt0)),
        'ssm_d': 1.0 + nrm(ks[13], (DEPTH, SSM_HEADS), 0.1),
        'ssm_norm_g': gain(ks[14], (DEPTH, SSM_D_INNER)),
        'sc_conv_w': nrm(ks[15], (DEPTH, SC_CONV, SC_WIDTH), SC_CONV ** -0.5),
        'mla_qa_g': gain(ks[16], (DEPTH, MLA_Q_RANK)),
        'mla_wqb': nrm(ks[17], (DEPTH, MLA_Q_RANK, MLA_HEADS * (MLA_NOPE + MLA_ROPE)), MLA_Q_RANK ** -0.5),
        'mla_kva_g': gain(ks[18], (DEPTH, MLA_KV_RANK)),
        'mla_wkvb': nrm(ks[19], (DEPTH, MLA_KV_RANK, MLA_HEADS * (MLA_NOPE + MLA_V)), MLA_KV_RANK ** -0.5),
        'w_branch': nrm(ks[20], (DEPTH, N_BRANCHES, BRANCH_WIDTH, D_MODEL), BRANCH_WIDTH ** -0.5),
        'w_out': nrm(ks[21], (DEPTH, D_MODEL, D_MODEL), D_MODEL ** -0.5),
        'mlp_w1': nrm(ks[22], (DEPTH, D_MODEL, D_FF), D_MODEL ** -0.5),
        'mlp_w2': nrm(ks[23], (DEPTH, D_FF, D_MODEL), D_FF ** -0.5),
        'final_norm_g': gain(ks[24], (D_MODEL,)),
    }


def reference(x, c, ctx, c_ctx, ada_w, ada_b, norm1_g, norm2_g, w_in, ssm_conv_w, ssm_conv_b, ssm_a_log, ssm_dt_bias, ssm_d, ssm_norm_g, sc_conv_w, mla_qa_g, mla_wqb, mla_kva_g, mla_wkvb, w_branch, w_out, mlp_w1, mlp_w2, final_norm_g):
    cos, sin = axial_rope(x.shape[1])
    x_lat, x_ctx = x, ctx
    for i in range(DEPTH):
        need_ctx = i < DEPTH - 1
        mod_lat = (jax.nn.silu(c) @ ada_w[i] + ada_b[i])[:, None, :]
        mod_ctx = jax.nn.silu(c_ctx) @ ada_w[i] + ada_b[i]
        sh1, sc1, g1, sh2, sc2, g2 = jnp.split(mod_lat, 6, axis=-1)
        csh1, csc1, cg1, csh2, csc2, cg2 = jnp.split(mod_ctx, 6, axis=-1)
        h_lat = modulate(rmsnorm(x_lat, norm1_g[i]), sh1, sc1)
        h_ctx = modulate(rmsnorm(x_ctx, norm1_g[i]), csh1, csc1)
        o_ctx, o_lat = hybrid_mixer(h_ctx, h_lat, cos, sin, w_in[i], ssm_conv_w[i], ssm_conv_b[i], ssm_a_log[i], ssm_dt_bias[i], ssm_d[i], ssm_norm_g[i], sc_conv_w[i], mla_qa_g[i], mla_wqb[i], mla_kva_g[i], mla_wkvb[i], w_branch[i], w_out[i], need_ctx)
        x_lat = x_lat + g1 * o_lat
        x_lat = x_lat + g2 * sq_relu_mlp(modulate(rmsnorm(x_lat, norm2_g[i]), sh2, sc2), mlp_w1[i], mlp_w2[i])
        if need_ctx:
            x_ctx = x_ctx + cg1 * o_ctx
            x_ctx = x_ctx + cg2 * sq_relu_mlp(modulate(rmsnorm(x_ctx, norm2_g[i]), csh2, csc2), mlp_w1[i], mlp_w2[i])
    return rmsnorm(x_lat, final_norm_g)
```

```python
import functools
import math

import jax
import jax.numpy as jnp
import numpy as np
from jax import lax
from jax.experimental import pallas as pl
from jax.experimental.pallas import tpu as pltpu

F32 = jnp.float32
BF16 = jnp.bfloat16

LANES = 128
BF16_SUBLANES = 16
VMEM_LIMIT_BYTES = 56 * 1024 * 1024

EPS = 1e-6
ROPE_THETA = 10000.0
GRID_W = 64
N_BRANCHES = 4
SSM_HEADS = 8
SSM_HEAD_DIM = 64
SSM_GROUPS = 2
SSM_HPG = SSM_HEADS // SSM_GROUPS
SSM_STATE = 128
SSM_CHUNK = 128
SSM_CONV = 5
SC_CONV = 3
MLA_HEADS = 8
MLA_NOPE = 64
MLA_ROPE = 32
MLA_V = 64
FNET_GROUPS = 4

ROW_TILE = 256
HALO = BF16_SUBLANES


def _const_spec(shape):
    nd = len(shape)
    return pl.BlockSpec(shape, lambda *_: (0,) * nd, pipeline_mode=pl.Buffered(1))


def _params(*sem):
    return pltpu.CompilerParams(dimension_semantics=sem, vmem_limit_bytes=VMEM_LIMIT_BYTES)


def _sigmoid(v):
    return 1.0 / (1.0 + jnp.exp(-v))


def _silu(v):
    return v * _sigmoid(v)


def _softplus(v):
    return jnp.maximum(v, 0.0) + jnp.log1p(jnp.exp(-jnp.abs(v)))


def _rms(v, g):
    return v * lax.rsqrt(jnp.mean(v * v, axis=-1, keepdims=True) + EPS) * g


def _dot(a, b):
    return jnp.dot(a, b, preferred_element_type=F32)


def _dot_exact(a, b):
    return jnp.dot(a, b, preferred_element_type=F32, precision=lax.Precision.HIGHEST)


def _dot_nt(a, b):
    return lax.dot_general(a, b, (((1,), (1,)), ((), ())), preferred_element_type=F32)


def _mod_kernel(c_ref, w_ref, b_ref, o_ref):
    o_ref[0, 0] = _dot_exact(_silu(c_ref[...]), w_ref[0]) + b_ref[0, 0]


def _modulation(cc, ada_w, ada_b):
    depth, d, d6 = ada_w.shape
    n = cc.shape[0]
    out = pl.pallas_call(
        _mod_kernel,
        grid=(depth, d6 // d),
        in_specs=[
            pl.BlockSpec((n, d), lambda i, k: (0, 0)),
            pl.BlockSpec((1, d, d), lambda i, k: (i, 0, k)),
            pl.BlockSpec((1, 1, 1, d), lambda i, k: (i, k, 0, 0)),
        ],
        out_specs=pl.BlockSpec((1, 1, n, d), lambda i, k: (i, k, 0, 0)),
        out_shape=jax.ShapeDtypeStruct((depth, d6 // d, n, d), F32),
        compiler_params=_params("arbitrary", "arbitrary"),
        name="adaln_mod",
    )(cc, ada_w, ada_b.reshape(depth, d6 // d, 1, d))
    return jnp.transpose(out, (0, 2, 1, 3))


_C_Z, _C_XBC, _C_SC, _C_QA, _C_KVA, _C_FN, _C_GT, _C_SM, _C_END = (
    0, 512, 1536, 3072, 3456, 3712, 4224, 8320, 8448)
_DT_LANE = 0
_KR_LANE = MLA_NOPE


def _rope(v, tab):
    tc = tab[:, 0:LANES]
    tsa = tab[:, LANES:2 * LANES]
    tsb = tab[:, 2 * LANES:3 * LANES]
    half = MLA_ROPE // 4
    return v * tc + pltpu.roll(v, LANES - half, 1) * tsa + pltpu.roll(v, half, 1) * tsb


def _pre_kernel(x_ref, mod_ref, g1_ref, w_ref, qag_ref, wqb_ref, kvag_ref, wkk_ref, wkv_ref,
                wc_ref, tab_ref,
                z_ref, xbc_ref, sc_ref, gt_ref, dt_ref, q_ref, k_ref, v_ref, uv_ref):
    x = x_ref[0]
    mod = mod_ref[0]
    h = _rms(x, g1_ref[...]) * (1.0 + mod[1:2]) + mod[0:1]
    hb = h.astype(BF16)

    def proj(lo, hi):
        return _dot(hb, w_ref[:, lo:hi])

    z_ref[0] = proj(_C_Z, _C_XBC).astype(BF16)
    xbc_ref[0] = proj(_C_XBC, _C_SC).astype(BF16)
    sc_ref[0] = proj(_C_SC, _C_QA).astype(BF16)
    gt_ref[0] = _sigmoid(proj(_C_GT, _C_SM)).astype(BF16)
    small = proj(_C_SM, _C_END)
    dt_ref[0] = small
    tab = tab_ref[...]

    qn = _rms(proj(_C_QA, _C_KVA), qag_ref[...]).astype(BF16)
    q = _dot(qn, wqb_ref[...])
    scale = (MLA_NOPE + MLA_ROPE) ** -0.5
    for hd in range(MLA_HEADS):
        sl = slice(hd * LANES, (hd + 1) * LANES)
        q_ref[0, :, sl] = (_rope(q[:, sl], tab) * scale).astype(BF16)

    kvn = _rms(proj(_C_KVA, _C_FN), kvag_ref[...]).astype(BF16)
    kn = _dot(kvn, wkk_ref[...])
    lane = lax.broadcasted_iota(jnp.int32, small.shape, 1)
    kr = _rope(jnp.where(lane >= _KR_LANE, small, 0.0), tab)
    for hd in range(MLA_HEADS):
        sl = slice(hd * LANES, (hd + 1) * LANES)
        k_ref[0, :, sl] = (kn[:, sl] + kr).astype(BF16)
    v_ref[0] = _dot(kvn, wkv_ref[...]).astype(BF16)

    fn = proj(_C_FN, _C_GT).astype(BF16)
    uv_ref[0] = _dot(fn, wc_ref[...]).astype(BF16)


def _pre(x, mod, g1, w1, qag, wqb, kvag, wkk, wkv, wc, tab, nct):
    b, t, d = x.shape
    nt = t // ROW_TILE
    nb = mod.shape[0] - 1

    def row(width, dtype=BF16):
        return (pl.BlockSpec((1, ROW_TILE, width), lambda bi, j: (bi, j, 0)),
                jax.ShapeDtypeStruct((b, t, width), dtype))

    outs = [row(512), row(1024), row(1536), row(4096), row(LANES, F32), row(1024), row(1024),
            row(512), row(1024)]
    return pl.pallas_call(
        _pre_kernel,
        grid=(b, nt),
        in_specs=[
            pl.BlockSpec((1, ROW_TILE, d), lambda bi, j: (bi, j, 0)),
            pl.BlockSpec((1, 6, d), lambda bi, j: (jnp.where(j < nct, nb, bi), 0, 0)),
            _const_spec(g1.shape), _const_spec(w1.shape), _const_spec(qag.shape),
            _const_spec(wqb.shape), _const_spec(kvag.shape), _const_spec(wkk.shape),
            _const_spec(wkv.shape), _const_spec(wc.shape),
            pl.BlockSpec((ROW_TILE, 3 * LANES), lambda bi, j: (j, 0)),
        ],
        out_specs=[o[0] for o in outs],
        out_shape=[o[1] for o in outs],
        compiler_params=_params("parallel", "arbitrary"),
        name="pre_proj",
    )(x, mod, g1, w1, qag, wqb, kvag, wkk, wkv, wc, tab)


def _halo_conv(top, mid, bot, top_ok, bot_ok, w, ksize):
    rows = mid.shape[0]
    full = jnp.concatenate([jnp.where(top_ok, top, 0.0), mid, jnp.where(bot_ok, bot, 0.0)], axis=0)
    n = full.shape[0]
    pad = (ksize - 1) // 2
    acc = None
    for k in range(ksize):
        d = k - pad
        sh = full if d == 0 else pltpu.roll(full, (-d) % n, 0)
        term = sh[HALO:HALO + rows] * w[k:k + 1]
        acc = term if acc is None else acc + term
    return acc


def _ssd_chunk(xs, bm, cm, dts, a_row, tri_a, tri_b, e_sel, st_ref, direction):
    off = SSM_HEADS * direction
    la = dts * a_row
    cum = _dot_exact(tri_a, la)
    cum_t = _dot_exact(la.T[off:off + SSM_HEADS], tri_b)
    cum_x = _dot_exact(cum, e_sel)
    dt_x = _dot_exact(dts, e_sel)
    last = cum.shape[0] - 1 if direction == 0 else 0
    tot_x = cum_x[last:last + 1]
    xdt = xs * dt_x
    xw = (xdt * jnp.exp(tot_x - cum_x)).astype(BF16)
    xdt_b = xdt.astype(BF16)
    keep = tri_a > 0.0
    lane = lax.broadcasted_iota(jnp.int32, (cum.shape[0], LANES), 1)
    first_half = lane < SSM_HEAD_DIM
    y_pairs = []
    y_off = []
    for g in range(SSM_GROUPS):
        cg = cm[:, g * SSM_STATE:(g + 1) * SSM_STATE]
        bg = bm[:, g * SSM_STATE:(g + 1) * SSM_STATE]
        cb = _dot_nt(cg, bg)
        yd = []
        for hl in range(SSM_HPG):
            hd = g * SSM_HPG + hl
            ci = cum[:, off + hd:off + hd + 1]
            cj = cum_t[hd:hd + 1, :]
            seg = jnp.exp(jnp.where(keep, ci - cj, -jnp.inf))
            m = (cb * seg).astype(BF16)
            pair = hd // 2
            yd.append(_dot(m, xdt_b[:, pair * LANES:(pair + 1) * LANES]))
        for pr in range(SSM_HPG // 2):
            y_pairs.append(jnp.where(first_half, yd[2 * pr], yd[2 * pr + 1]))
        cols = slice(g * SSM_HPG * SSM_HEAD_DIM, (g + 1) * SSM_HPG * SSM_HEAD_DIM)
        st = st_ref[:, cols]
        y_off.append(_dot(cg, st.astype(BF16)))
        upd = _dot(bg.astype(F32).T.astype(BF16), xw[:, cols])
        st_ref[:, cols] = st * jnp.exp(tot_x[:, cols]) + upd
    y_diag = jnp.concatenate(y_pairs, axis=1)
    return y_diag + jnp.concatenate(y_off, axis=1) * jnp.exp(cum_x)


def _ssd_f_kernel(nct, nt, xbc_ref, top_ref, bot_ref, dt_ref, cw_ref, cb_ref, alog_ref, dtb_ref,
                  tri_ref, trit_ref, e_ref, xact_ref, yf_ref, st_ref):
    j = pl.program_id(1)

    @pl.when(j == 0)
    def _():
        st_ref[...] = jnp.zeros_like(st_ref)

    top_ok = jnp.logical_and(j != 0, j != nct)
    bot_ok = jnp.logical_and(j != nct - 1, j != nt - 1)
    conv = _halo_conv(top_ref[0].astype(F32), xbc_ref[0].astype(F32), bot_ref[0].astype(F32),
                      top_ok, bot_ok, cw_ref[...], SSM_CONV)
    act = _silu(conv + cb_ref[...])
    actb = act.astype(BF16)
    xact_ref[0] = actb
    dts = _softplus(dt_ref[0] + dtb_ref[...])
    a_row = -jnp.exp(alog_ref[...])
    d_in = SSM_HEADS * SSM_HEAD_DIM
    for c in range(ROW_TILE // SSM_CHUNK):
        r = slice(c * SSM_CHUNK, (c + 1) * SSM_CHUNK)
        yf_ref[0, r, :] = _ssd_chunk(act[r, 0:d_in], actb[r, d_in:d_in + 256], actb[r, d_in + 256:],
                                     dts[r], a_row, tri_ref[...], trit_ref[...], e_ref[0], st_ref, 0)


def _ssd_b_kernel(xact_ref, dt_ref, z_ref, yf_ref, alog_ref, dtb_ref, dsk_ref, ng_ref,
                  tri_ref, trit_ref, e_ref, y_ref, st_ref):
    j = pl.program_id(1)

    @pl.when(j == 0)
    def _():
        st_ref[...] = jnp.zeros_like(st_ref)

    actb = xact_ref[0]
    dts = _softplus(dt_ref[0] + dtb_ref[...])
    a_row = -jnp.exp(alog_ref[...])
    d_in = SSM_HEADS * SSM_HEAD_DIM
    half = d_in // SSM_GROUPS
    for c in reversed(range(ROW_TILE // SSM_CHUNK)):
        r = slice(c * SSM_CHUNK, (c + 1) * SSM_CHUNK)
        xs = actb[r, 0:d_in].astype(F32)
        yb = _ssd_chunk(xs, actb[r, d_in:d_in + 256], actb[r, d_in + 256:], dts[r], a_row,
                        trit_ref[...], tri_ref[...], e_ref[1], st_ref, 1)
        y = (yf_ref[0, r, :] + yb + dsk_ref[...] * xs) * _silu(z_ref[0, r, :].astype(F32))
        for g in range(SSM_GROUPS):
            cols = slice(g * half, (g + 1) * half)
            y_ref[0, r, cols] = _rms(y[:, cols], ng_ref[:, cols]).astype(BF16)


def _ssd(xbc, dt, z, cw, cb, alog, dtb, dsk, ng, tri, trit, e_sel, nct):
    b, t, c = xbc.shape
    nt = t // ROW_TILE
    hb = ROW_TILE // HALO
    nh = t // HALO
    d_in = SSM_HEADS * SSM_HEAD_DIM
    tile = lambda width: pl.BlockSpec((1, ROW_TILE, width), lambda bi, j: (bi, j, 0))
    xact, yf = pl.pallas_call(
        functools.partial(_ssd_f_kernel, nct, nt),
        grid=(b, nt),
        in_specs=[
            tile(c),
            pl.BlockSpec((1, HALO, c), lambda bi, j: (bi, jnp.maximum(j * hb - 1, 0), 0)),
            pl.BlockSpec((1, HALO, c), lambda bi, j: (bi, jnp.minimum((j + 1) * hb, nh - 1), 0)),
            tile(LANES),
            _const_spec(cw.shape), _const_spec(cb.shape), _const_spec(alog.shape),
            _const_spec(dtb.shape), _const_spec(tri.shape), _const_spec(trit.shape),
            _const_spec(e_sel.shape),
        ],
        out_specs=[tile(c), tile(d_in)],
        out_shape=[jax.ShapeDtypeStruct((b, t, c), BF16), jax.ShapeDtypeStruct((b, t, d_in), F32)],
        scratch_shapes=[pltpu.VMEM((SSM_STATE, d_in), F32)],
        compiler_params=_params("parallel", "arbitrary"),
        name="ssd_fwd",
    )(xbc, xbc, xbc, dt, cw, cb, alog, dtb, tri, trit, e_sel)

    def rev(j):
        return jnp.where(j < nct, nct - 1 - j, nt - 1 - (j - nct))

    rtile = lambda width: pl.BlockSpec((1, ROW_TILE, width), lambda bi, j: (bi, rev(j), 0))
    return pl.pallas_call(
        _ssd_b_kernel,
        grid=(b, nt),
        in_specs=[
            rtile(c), rtile(LANES), rtile(d_in), rtile(d_in),
            _const_spec(alog.shape), _const_spec(dtb.shape), _const_spec(dsk.shape),
            _const_spec(ng.shape), _const_spec(tri.shape), _const_spec(trit.shape),
            _const_spec(e_sel.shape),
        ],
        out_specs=rtile(d_in),
        out_shape=jax.ShapeDtypeStruct((b, t, d_in), BF16),
        scratch_shapes=[pltpu.VMEM((SSM_STATE, d_in), F32)],
        compiler_params=_params("parallel", "arbitrary"),
        name="ssd_bwd",
    )(xact, dt, z, yf, alog, dtb, dsk, ng, tri, trit, e_sel)


def _attn_kernel(nct, q_ref, k_ref, v_ref, o_ref):
    j = pl.program_id(2)
    rows = q_ref.shape[1]
    first = lax.broadcasted_iota(jnp.int32, (rows, LANES), 1) < MLA_V

    def attend(nk):
        outs = []
        for hh in range(2):
            sl = slice(hh * LANES, (hh + 1) * LANES)
            s = _dot_nt(q_ref[0, :, sl], k_ref[0, 0:nk, sl])
            p = jnp.exp(s - jnp.max(s, axis=-1, keepdims=True))
            l = jnp.sum(p, axis=-1, keepdims=True)
            outs.append(_dot(p.astype(BF16), v_ref[0, 0:nk, :]) / l)
        o_ref[0] = jnp.where(first, outs[0], outs[1]).astype(BF16)

    @pl.when(j < nct)
    def _():
        attend(nct * ROW_TILE)

    @pl.when(j >= nct)
    def _():
        attend(k_ref.shape[1])


def _attention(q, k, v, nct):
    b, t, _ = q.shape
    nt = t // ROW_TILE
    pairs = MLA_HEADS // 2
    return pl.pallas_call(
        functools.partial(_attn_kernel, nct),
        grid=(b, pairs, nt),
        in_specs=[
            pl.BlockSpec((1, ROW_TILE, 2 * LANES), lambda bi, p, j: (bi, j, p)),
            pl.BlockSpec((1, t, 2 * LANES), lambda bi, p, j: (bi, 0, p)),
            pl.BlockSpec((1, t, LANES), lambda bi, p, j: (bi, 0, p)),
        ],
        out_specs=pl.BlockSpec((1, ROW_TILE, LANES), lambda bi, p, j: (bi, j, p)),
        out_shape=jax.ShapeDtypeStruct((b, t, pairs * LANES), BF16),
        compiler_params=_params("parallel", "arbitrary", "arbitrary"),
        name="mla_attn",
    )(q, k, v)


def _fmix_kernel(nct, uv_ref, cl_ref, sl_ref, cc_ref, sc_ref, o_ref):
    j = pl.program_id(1)
    half = uv_ref.shape[2] // 2
    nctx = nct * ROW_TILE

    @pl.when(j < nct)
    def _():
        o_ref[0] = (_dot(cc_ref[...], uv_ref[0, 0:nctx, 0:half])
                    + _dot(sc_ref[...], uv_ref[0, 0:nctx, half:])).astype(BF16)

    @pl.when(j >= nct)
    def _():
        o_ref[0] = (_dot(cl_ref[...], uv_ref[0, nctx:, 0:half])
                    + _dot(sl_ref[...], uv_ref[0, nctx:, half:])).astype(BF16)


def _fmix(uv, cl, sl, cc, sc, nct):
    b, t, w = uv.shape
    nt = t // ROW_TILE
    nlat = cl.shape[1]
    nctx = cc.shape[1]
    lat_blk = lambda bi, j: (jnp.maximum(j - nct, 0), 0)
    ctx_blk = lambda bi, j: (jnp.minimum(j, nct - 1), 0)
    return pl.pallas_call(
        functools.partial(_fmix_kernel, nct),
        grid=(b, nt),
        in_specs=[
            pl.BlockSpec((1, t, w), lambda bi, j: (bi, 0, 0)),
            pl.BlockSpec((ROW_TILE, nlat), lat_blk),
            pl.BlockSpec((ROW_TILE, nlat), lat_blk),
            pl.BlockSpec((ROW_TILE, nctx), ctx_blk),
            pl.BlockSpec((ROW_TILE, nctx), ctx_blk),
        ],
        out_specs=pl.BlockSpec((1, ROW_TILE, w // 2), lambda bi, j: (bi, j, 0)),
        out_shape=jax.ShapeDtypeStruct((b, t, w // 2), BF16),
        compiler_params=_params("parallel", "arbitrary"),
        name="fourier_mix",
    )(uv, cl, sl, cc, sc)


def _post_kernel(nct, nt, x_ref, mod_ref, ys_ref, sc_ref, top_ref, bot_ref, at_ref, fo_ref, gt_ref,
                 scw_ref, wb_ref, wo_ref, g2_ref, w1_ref, w2_ref, o_ref):
    j = pl.program_id(1)
    mod = mod_ref[0]
    w = sc_ref.shape[2] // 3
    top_ok = jnp.logical_and(j != 0, j != nct)
    bot_ok = jnp.logical_and(j != nct - 1, j != nt - 1)

    def cx(ref):
        return ref[0, :, w:2 * w].astype(F32) * ref[0, :, 2 * w:3 * w].astype(F32)

    conv = _halo_conv(cx(top_ref), cx(sc_ref), cx(bot_ref), top_ok, bot_ok, scw_ref[...], SC_CONV)
    branches = [ys_ref[0], (sc_ref[0, :, 0:w].astype(F32) * conv).astype(BF16), at_ref[0], fo_ref[0]]
    d = x_ref.shape[2]
    merged = None
    for k in range(N_BRANCHES):
        term = gt_ref[0, :, k * d:(k + 1) * d].astype(F32) * _dot(branches[k], wb_ref[k])
        merged = term if merged is None else merged + term
    x1 = x_ref[0] + mod[2:3] * _dot(merged.astype(BF16), wo_ref[...])
    h2 = (_rms(x1, g2_ref[...]) * (1.0 + mod[4:5]) + mod[3:4]).astype(BF16)
    hid = jnp.maximum(_dot(h2, w1_ref[...]), 0.0)
    o_ref[0] = x1 + mod[5:6] * _dot((hid * hid).astype(BF16), w2_ref[...])


def _post(x, mod, ys, sc, at, fo, gt, scw, wb, wo, g2, w1, w2, nct):
    b, t, d = x.shape
    nt = t // ROW_TILE
    nb = mod.shape[0] - 1
    hb = ROW_TILE // HALO
    nh = t // HALO
    tile = lambda width: pl.BlockSpec((1, ROW_TILE, width), lambda bi, j: (bi, j, 0))
    scw_w = sc.shape[2]
    return pl.pallas_call(
        functools.partial(_post_kernel, nct, nt),
        grid=(b, nt),
        in_specs=[
            tile(d),
            pl.BlockSpec((1, 6, d), lambda bi, j: (jnp.where(j < nct, nb, bi), 0, 0)),
            tile(ys.shape[2]), tile(scw_w),
            pl.BlockSpec((1, HALO, scw_w), lambda bi, j: (bi, jnp.maximum(j * hb - 1, 0), 0)),
            pl.BlockSpec((1, HALO, scw_w), lambda bi, j: (bi, jnp.minimum((j + 1) * hb, nh - 1), 0)),
            tile(at.shape[2]), tile(fo.shape[2]), tile(gt.shape[2]),
            _const_spec(scw.shape), _const_spec(wb.shape), _const_spec(wo.shape),
            _const_spec(g2.shape), _const_spec(w1.shape), _const_spec(w2.shape),
        ],
        out_specs=tile(d),
        out_shape=jax.ShapeDtypeStruct((b, t, d), F32),
        input_output_aliases={0: 0},
        compiler_params=_params("parallel", "arbitrary"),
        name="post_merge_mlp",
    )(x, mod, ys, sc, sc, sc, at, fo, gt, scw, wb, wo, g2, w1, w2)


def _final_kernel(x_ref, g_ref, o_ref):
    o_ref[0] = _rms(x_ref[0], g_ref[...])


def _final_norm(x, g, nct, seq):
    b, t, d = x.shape
    return pl.pallas_call(
        _final_kernel,
        grid=(b, seq // ROW_TILE),
        in_specs=[pl.BlockSpec((1, ROW_TILE, d), lambda bi, j: (bi, j + nct, 0)),
                  _const_spec(g.shape)],
        out_specs=pl.BlockSpec((1, ROW_TILE, d), lambda bi, j: (bi, j, 0)),
        out_shape=jax.ShapeDtypeStruct((b, seq, d), F32),
        compiler_params=_params("parallel", "arbitrary"),
        name="final_norm",
    )(x, g)


def _rope_table(seq, ctx):
    nf = MLA_ROPE // 4
    pos = jnp.arange(seq)
    inv = ROPE_THETA ** (-jnp.arange(nf, dtype=F32) / nf)
    ang = jnp.stack([(pos // GRID_W).astype(F32)[:, None] * inv,
                     (pos % GRID_W).astype(F32)[:, None] * inv], axis=1)
    cos, sin = jnp.cos(ang), jnp.sin(ang)
    zero = jnp.zeros_like(sin)
    cos_l = jnp.stack([cos, cos], axis=2).reshape(seq, MLA_ROPE)
    sa_l = jnp.stack([-sin, zero], axis=2).reshape(seq, MLA_ROPE)
    sb_l = jnp.stack([zero, sin], axis=2).reshape(seq, MLA_ROPE)

    def lanes(v, fill):
        full = jnp.full((ctx + seq, LANES), fill, F32)
        return full.at[ctx:, _KR_LANE:_KR_LANE + MLA_ROPE].set(v)

    return jnp.concatenate([lanes(cos_l, 1.0), lanes(sa_l, 0.0), lanes(sb_l, 0.0)], axis=1)


def _dft_mats(n):
    idx = jnp.arange(n, dtype=jnp.int32)
    ang = ((idx[:, None] * idx[None, :]) % n).astype(F32) * (2.0 * math.pi / n)
    s = n ** -0.5
    return (jnp.cos(ang) * s).astype(BF16), (-jnp.sin(ang) * s).astype(BF16)


def _channel_dft(width):
    gd = width // FNET_GROUPS
    idx = np.arange(gd)
    ang = 2.0 * np.pi * ((idx[:, None] * idx[None, :]) % gd) / gd
    eye = np.eye(FNET_GROUPS)
    c = np.kron(eye, np.cos(ang)) / math.sqrt(gd)
    s = np.kron(eye, np.sin(ang)) / math.sqrt(gd)
    return jnp.asarray(np.concatenate([c, s], axis=1), BF16)


def _head_pad(w, width, used):
    k = w.shape[0]
    w = w.reshape(k, MLA_HEADS, width)[:, :, :used]
    return jnp.pad(w, ((0, 0), (0, 0), (0, LANES - used))).reshape(k, MLA_HEADS * LANES)


def kernel(x, c, ctx, c_ctx, ada_w, ada_b, norm1_g, norm2_g, w_in, ssm_conv_w, ssm_conv_b, ssm_a_log, ssm_dt_bias, ssm_d, ssm_norm_g, sc_conv_w, mla_qa_g, mla_wqb, mla_kva_g, mla_wkvb, w_branch, w_out, mlp_w1, mlp_w2, final_norm_g):
    b, seq, d = x.shape
    nctx = ctx.shape[1]
    depth = ada_w.shape[0]
    assert seq % ROW_TILE == 0 and nctx % ROW_TILE == 0 and seq % GRID_W == 0
    nct = nctx // ROW_TILE
    d_in = SSM_HEADS * SSM_HEAD_DIM

    xcat = jnp.concatenate([ctx, x], axis=1)
    mods = _modulation(jnp.concatenate([c, c_ctx[None]], axis=0), ada_w, ada_b)

    tab = _rope_table(seq, nctx)
    cl, sl = _dft_mats(seq)
    cc, sc_m = _dft_mats(nctx)
    wc = _channel_dft(d // 2)
    tri = jnp.tril(jnp.ones((SSM_CHUNK, SSM_CHUNK), F32))
    trit = tri.T
    e_np = np.zeros((2, LANES, d_in), np.float32)
    for dr in range(2):
        for hd in range(SSM_HEADS):
            e_np[dr, SSM_HEADS * dr + hd, hd * SSM_HEAD_DIM:(hd + 1) * SSM_HEAD_DIM] = 1.0
    e_sel = jnp.asarray(e_np)

    def lane_pad(v):
        return jnp.pad(v.reshape(1, -1), ((0, 0), (0, LANES - v.size)))

    for i in range(depth):
        wz, wxbc, wdt, wsc, wqa, wkva, wkr, wfn, wgt = jnp.split(
            w_in[i], np.cumsum([512, 1024, 16, 1536, 384, 256, 32, 512])[:].tolist(), axis=1)
        small = jnp.zeros((d, LANES), F32)
        small = small.at[:, _DT_LANE:_DT_LANE + 16].set(wdt).at[:, _KR_LANE:_KR_LANE + MLA_ROPE].set(wkr)
        w1 = jnp.concatenate([wz, wxbc, wsc, wqa, wkva, wfn, wgt, small], axis=1).astype(BF16)
        wqb = _head_pad(mla_wqb[i], MLA_NOPE + MLA_ROPE, MLA_NOPE + MLA_ROPE).astype(BF16)
        wkk = _head_pad(mla_wkvb[i], MLA_NOPE + MLA_V, MLA_NOPE).astype(BF16)
        wkv = mla_wkvb[i].reshape(-1, MLA_HEADS, MLA_NOPE + MLA_V)[:, :, MLA_NOPE:]
        wkv = wkv.reshape(-1, MLA_HEADS * MLA_V).astype(BF16)

        z, xbc, sc, gt, dt, q, k, v, uv = _pre(
            xcat, mods[i], norm1_g[i][None], w1, mla_qa_g[i][None], wqb, mla_kva_g[i][None], wkk, wkv,
            wc, tab, nct)
        ys = _ssd(xbc, dt, z, ssm_conv_w[i], ssm_conv_b[i][None], lane_pad(ssm_a_log[i]),
                  lane_pad(ssm_dt_bias[i]), jnp.repeat(ssm_d[i], SSM_HEAD_DIM)[None],
                  ssm_norm_g[i][None], tri, trit, e_sel, nct)
        at = _attention(q, k, v, nct)
        fo = _fmix(uv, cl, sl, cc, sc_m, nct)
        xcat = _post(xcat, mods[i], ys, sc, at, fo, gt, sc_conv_w[i], w_branch[i].astype(BF16),
                     w_out[i].astype(BF16), norm2_g[i][None], mlp_w1[i].astype(BF16),
                     mlp_w2[i].astype(BF16), nct)
    return _final_norm(xcat, final_norm_g[None], nct, seq)
```

```python
import functools
import math

import jax
import jax.numpy as jnp
import numpy as np
from jax import lax
from jax.experimental import pallas as pl
from jax.experimental.pallas import tpu as pltpu

F32 = jnp.float32
BF16 = jnp.bfloat16

LANES = 128
BF16_SUBLANES = 16
VMEM_LIMIT_BYTES = 56 * 1024 * 1024

EPS = 1e-6
ROPE_THETA = 10000.0
GRID_W = 64
N_BRANCHES = 4
SSM_HEADS = 8
SSM_HEAD_DIM = 64
SSM_GROUPS = 2
SSM_HPG = SSM_HEADS // SSM_GROUPS
SSM_STATE = 128
SSM_CHUNK = 128
SSM_CONV = 5
SC_CONV = 3
MLA_HEADS = 8
MLA_NOPE = 64
MLA_ROPE = 32
MLA_V = 64
FNET_GROUPS = 4

ROW_TILE = 256
HALO = BF16_SUBLANES
ATTN_TILES_PER_STEP = (3, 2, 1)


def _const_spec(shape):
    nd = len(shape)
    return pl.BlockSpec(shape, lambda *_: (0,) * nd, pipeline_mode=pl.Buffered(1))


def _params(*sem):
    return pltpu.CompilerParams(dimension_semantics=sem, vmem_limit_bytes=VMEM_LIMIT_BYTES)


def _sigmoid(v):
    return 1.0 / (1.0 + jnp.exp(-v))


def _silu(v):
    return v * _sigmoid(v)


def _softplus(v):
    return jnp.maximum(v, 0.0) + jnp.log1p(jnp.exp(-jnp.abs(v)))


def _rms(v, g):
    return v * lax.rsqrt(jnp.mean(v * v, axis=-1, keepdims=True) + EPS) * g


def _dot(a, b):
    return jnp.dot(a, b, preferred_element_type=F32)


def _dot_exact(a, b):
    return jnp.dot(a, b, preferred_element_type=F32, precision=lax.Precision.HIGHEST)


def _dot_nt(a, b):
    return lax.dot_general(a, b, (((1,), (1,)), ((), ())), preferred_element_type=F32)


def _mod_kernel(c_ref, w_ref, b_ref, o_ref):
    o_ref[0, 0] = _dot_exact(_silu(c_ref[...]), w_ref[0]) + b_ref[0, 0]


def _modulation(cc, ada_w, ada_b):
    depth, d, d6 = ada_w.shape
    n = cc.shape[0]
    out = pl.pallas_call(
        _mod_kernel,
        grid=(depth, d6 // d),
        in_specs=[
            pl.BlockSpec((n, d), lambda i, k: (0, 0)),
            pl.BlockSpec((1, d, d), lambda i, k: (i, 0, k)),
            pl.BlockSpec((1, 1, 1, d), lambda i, k: (i, k, 0, 0)),
        ],
        out_specs=pl.BlockSpec((1, 1, n, d), lambda i, k: (i, k, 0, 0)),
        out_shape=jax.ShapeDtypeStruct((depth, d6 // d, n, d), F32),
        compiler_params=_params("arbitrary", "arbitrary"),
        name="adaln_mod",
    )(cc, ada_w, ada_b.reshape(depth, d6 // d, 1, d))
    return jnp.transpose(out, (0, 2, 1, 3))


_C_Z, _C_XBC, _C_SC, _C_QA, _C_KVA, _C_FN, _C_GT, _C_SM, _C_END = (
    0, 512, 1536, 3072, 3456, 3712, 4224, 8320, 8448)
_DT_LANE = 0
_KR_LANE = MLA_NOPE


def _rope(v, tab):
    tc = tab[:, 0:LANES]
    tsa = tab[:, LANES:2 * LANES]
    tsb = tab[:, 2 * LANES:3 * LANES]
    half = MLA_ROPE // 4
    return v * tc + pltpu.roll(v, LANES - half, 1) * tsa + pltpu.roll(v, half, 1) * tsb


def _pre_kernel(x_ref, mod_ref, g1_ref, w_ref, qag_ref, wqb_ref, kvag_ref, wkk_ref, wkv_ref,
                wc_ref, tab_ref,
                z_ref, xbc_ref, sc_ref, gt_ref, dt_ref, q_ref, k_ref, v_ref, uv_ref):
    x = x_ref[0]
    mod = mod_ref[0]
    h = _rms(x, g1_ref[...]) * (1.0 + mod[1:2]) + mod[0:1]
    hb = h.astype(BF16)

    def proj(lo, hi):
        return _dot(hb, w_ref[:, lo:hi])

    z_ref[0] = proj(_C_Z, _C_XBC).astype(BF16)
    xbc_ref[0] = proj(_C_XBC, _C_SC).astype(BF16)
    sc_ref[0] = proj(_C_SC, _C_QA).astype(BF16)
    gt_ref[0] = _sigmoid(proj(_C_GT, _C_SM)).astype(BF16)
    small = proj(_C_SM, _C_END)
    dt_ref[0] = small
    tab = tab_ref[...]

    qn = _rms(proj(_C_QA, _C_KVA), qag_ref[...]).astype(BF16)
    q = _dot(qn, wqb_ref[...])
    scale = (MLA_NOPE + MLA_ROPE) ** -0.5 * math.log2(math.e)
    for hd in range(MLA_HEADS):
        sl = slice(hd * LANES, (hd + 1) * LANES)
        q_ref[0, :, sl] = (_rope(q[:, sl], tab) * scale).astype(BF16)

    kvn = _rms(proj(_C_KVA, _C_FN), kvag_ref[...]).astype(BF16)
    kn = _dot(kvn, wkk_ref[...])
    lane = lax.broadcasted_iota(jnp.int32, small.shape, 1)
    kr = _rope(jnp.where(lane >= _KR_LANE, small, 0.0), tab)
    for hd in range(MLA_HEADS):
        sl = slice(hd * LANES, (hd + 1) * LANES)
        k_ref[0, :, sl] = (kn[:, sl] + kr).astype(BF16)
    v_ref[0] = _dot(kvn, wkv_ref[...]).astype(BF16)

    fn = proj(_C_FN, _C_GT).astype(BF16)
    uv_ref[0] = _dot(fn, wc_ref[...]).astype(BF16)


def _pre(x, mod, g1, w1, qag, wqb, kvag, wkk, wkv, wc, tab, nct):
    b, t, d = x.shape
    nt = t // ROW_TILE
    nb = mod.shape[0] - 1

    def row(width, dtype=BF16):
        return (pl.BlockSpec((1, ROW_TILE, width), lambda bi, j: (bi, j, 0)),
                jax.ShapeDtypeStruct((b, t, width), dtype))

    outs = [row(512), row(1024), row(1536), row(4096), row(LANES, F32), row(1024), row(1024),
            row(512), row(1024)]
    return pl.pallas_call(
        _pre_kernel,
        grid=(b, nt),
        in_specs=[
            pl.BlockSpec((1, ROW_TILE, d), lambda bi, j: (bi, j, 0)),
            pl.BlockSpec((1, 6, d), lambda bi, j: (jnp.where(j < nct, nb, bi), 0, 0)),
            _const_spec(g1.shape), _const_spec(w1.shape), _const_spec(qag.shape),
            _const_spec(wqb.shape), _const_spec(kvag.shape), _const_spec(wkk.shape),
            _const_spec(wkv.shape), _const_spec(wc.shape),
            pl.BlockSpec((ROW_TILE, 3 * LANES), lambda bi, j: (j, 0)),
        ],
        out_specs=[o[0] for o in outs],
        out_shape=[o[1] for o in outs],
        compiler_params=_params("parallel", "arbitrary"),
        name="pre_proj",
    )(x, mod, g1, w1, qag, wqb, kvag, wkk, wkv, wc, tab)


def _halo_conv(top, mid, bot, top_ok, bot_ok, w, ksize):
    rows = mid.shape[0]
    full = jnp.concatenate([jnp.where(top_ok, top, 0.0), mid, jnp.where(bot_ok, bot, 0.0)], axis=0)
    n = full.shape[0]
    pad = (ksize - 1) // 2
    acc = None
    for k in range(ksize):
        d = k - pad
        sh = full if d == 0 else pltpu.roll(full, (-d) % n, 0)
        term = sh[HALO:HALO + rows] * w[k:k + 1]
        acc = term if acc is None else acc + term
    return acc


def _split3(v, axis):
    hi = v.astype(BF16)
    r1 = v - hi.astype(F32)
    mid = r1.astype(BF16)
    lo = (r1 - mid.astype(F32)).astype(BF16)
    return jnp.concatenate([hi, mid, lo], axis=axis)


def _ssd_chunk(xs, bm, cm, dts, a_row, tri3, e_wide, e_chan, st_ref, direction):
    q = xs.shape[0]
    off = SSM_HEADS * direction
    la = dts * a_row
    cum = _dot(tri3, _split3(la, 0))
    cum_t = cum.T[off:off + SSM_HEADS]
    cum_w = _dot(_split3(cum, 1), e_wide)
    dt_x = _dot(_split3(dts, 1), e_chan)
    lane = lax.broadcasted_iota(jnp.int32, (q, LANES), 1)
    first_half = lane < SSM_HEAD_DIM
    cum_x = jnp.concatenate(
        [jnp.where(first_half, cum_w[:, (2 * p) * LANES:(2 * p + 1) * LANES],
                   cum_w[:, (2 * p + 1) * LANES:(2 * p + 2) * LANES]) for p in range(SSM_HEADS // 2)],
        axis=1)
    last = q - 1 if direction == 0 else 0
    tot_x = cum_x[last:last + 1]
    xdt = xs * dt_x
    xw = (xdt * jnp.exp2(tot_x - cum_x)).astype(BF16)
    xdt_b = xdt.astype(BF16)
    keep = tri3[:, 0:q] > 0
    y_pairs = []
    y_off = []
    for g in range(SSM_GROUPS):
        cg = cm[:, g * SSM_STATE:(g + 1) * SSM_STATE]
        bg = bm[:, g * SSM_STATE:(g + 1) * SSM_STATE]
        cb = _dot_nt(cg, bg)
        yd = []
        for hl in range(SSM_HPG):
            hd = g * SSM_HPG + hl
            ci = cum_w[:, hd * LANES:(hd + 1) * LANES]
            cj = cum_t[hd:hd + 1, :]
            seg = jnp.exp2(jnp.where(keep, ci - cj, -jnp.inf))
            m = (cb * seg).astype(BF16)
            pair = hd // 2
            yd.append(_dot(m, xdt_b[:, pair * LANES:(pair + 1) * LANES]))
        for pr in range(SSM_HPG // 2):
            y_pairs.append(jnp.where(first_half, yd[2 * pr], yd[2 * pr + 1]))
        cols = slice(g * SSM_HPG * SSM_HEAD_DIM, (g + 1) * SSM_HPG * SSM_HEAD_DIM)
        st = st_ref[:, cols]
        y_off.append(_dot(cg, st.astype(BF16)))
        upd = _dot(bg.astype(F32).T.astype(BF16), xw[:, cols])
        st_ref[:, cols] = st * jnp.exp2(tot_x[:, cols]) + upd
    y_diag = jnp.concatenate(y_pairs, axis=1)
    return y_diag + jnp.concatenate(y_off, axis=1) * jnp.exp2(cum_x)


def _ssd_f_kernel(nct, nt, xbc_ref, top_ref, bot_ref, dt_ref, cw_ref, cb_ref, alog_ref, dtb_ref,
                  tri_ref, ew_ref, ec_ref, xact_ref, yf_ref, st_ref):
    j = pl.program_id(1)

    @pl.when(j == 0)
    def _():
        st_ref[...] = jnp.zeros_like(st_ref)

    top_ok = jnp.logical_and(j != 0, j != nct)
    bot_ok = jnp.logical_and(j != nct - 1, j != nt - 1)
    conv = _halo_conv(top_ref[0].astype(F32), xbc_ref[0].astype(F32), bot_ref[0].astype(F32),
                      top_ok, bot_ok, cw_ref[...], SSM_CONV)
    act = _silu(conv + cb_ref[...])
    actb = act.astype(BF16)
    xact_ref[0] = actb
    dts = _softplus(dt_ref[0] + dtb_ref[...])
    a_row = -jnp.exp(alog_ref[...]) * math.log2(math.e)
    d_in = SSM_HEADS * SSM_HEAD_DIM
    for c in range(ROW_TILE // SSM_CHUNK):
        r = slice(c * SSM_CHUNK, (c + 1) * SSM_CHUNK)
        yf_ref[0, r, :] = _ssd_chunk(act[r, 0:d_in], actb[r, d_in:d_in + 256], actb[r, d_in + 256:],
                                     dts[r], a_row, tri_ref[0], ew_ref[0], ec_ref[0], st_ref, 0)


def _ssd_b_kernel(xact_ref, dt_ref, z_ref, yf_ref, alog_ref, dtb_ref, dsk_ref, ng_ref,
                  tri_ref, ew_ref, ec_ref, y_ref, st_ref):
    j = pl.program_id(1)

    @pl.when(j == 0)
    def _():
        st_ref[...] = jnp.zeros_like(st_ref)

    actb = xact_ref[0]
    dts = _softplus(dt_ref[0] + dtb_ref[...])
    a_row = -jnp.exp(alog_ref[...]) * math.log2(math.e)
    d_in = SSM_HEADS * SSM_HEAD_DIM
    half = d_in // SSM_GROUPS
    for c in reversed(range(ROW_TILE // SSM_CHUNK)):
        r = slice(c * SSM_CHUNK, (c + 1) * SSM_CHUNK)
        xs = actb[r, 0:d_in].astype(F32)
        yb = _ssd_chunk(xs, actb[r, d_in:d_in + 256], actb[r, d_in + 256:], dts[r], a_row,
                        tri_ref[1], ew_ref[1], ec_ref[1], st_ref, 1)
        y = (yf_ref[0, r, :] + yb + dsk_ref[...] * xs) * _silu(z_ref[0, r, :].astype(F32))
        for g in range(SSM_GROUPS):
            cols = slice(g * half, (g + 1) * half)
            y_ref[0, r, cols] = _rms(y[:, cols], ng_ref[:, cols]).astype(BF16)


def _ssd(xbc, dt, z, cw, cb, alog, dtb, dsk, ng, tri3, e_wide, e_chan, nct):
    b, t, c = xbc.shape
    nt = t // ROW_TILE
    hb = ROW_TILE // HALO
    nh = t // HALO
    d_in = SSM_HEADS * SSM_HEAD_DIM
    tile = lambda width: pl.BlockSpec((1, ROW_TILE, width), lambda bi, j: (bi, j, 0))
    xact, yf = pl.pallas_call(
        functools.partial(_ssd_f_kernel, nct, nt),
        grid=(b, nt),
        in_specs=[
            tile(c),
            pl.BlockSpec((1, HALO, c), lambda bi, j: (bi, jnp.maximum(j * hb - 1, 0), 0)),
            pl.BlockSpec((1, HALO, c), lambda bi, j: (bi, jnp.minimum((j + 1) * hb, nh - 1), 0)),
            tile(LANES),
            _const_spec(cw.shape), _const_spec(cb.shape), _const_spec(alog.shape),
            _const_spec(dtb.shape), _const_spec(tri3.shape), _const_spec(e_wide.shape),
            _const_spec(e_chan.shape),
        ],
        out_specs=[tile(c), tile(d_in)],
        out_shape=[jax.ShapeDtypeStruct((b, t, c), BF16), jax.ShapeDtypeStruct((b, t, d_in), F32)],
        scratch_shapes=[pltpu.VMEM((SSM_STATE, d_in), F32)],
        compiler_params=_params("parallel", "arbitrary"),
        name="ssd_fwd",
    )(xbc, xbc, xbc, dt, cw, cb, alog, dtb, tri3, e_wide, e_chan)

    def rev(j):
        return jnp.where(j < nct, nct - 1 - j, nt - 1 - (j - nct))

    rtile = lambda width: pl.BlockSpec((1, ROW_TILE, width), lambda bi, j: (bi, rev(j), 0))
    return pl.pallas_call(
        _ssd_b_kernel,
        grid=(b, nt),
        in_specs=[
            rtile(c), rtile(LANES), rtile(d_in), rtile(d_in),
            _const_spec(alog.shape), _const_spec(dtb.shape), _const_spec(dsk.shape),
            _const_spec(ng.shape), _const_spec(tri3.shape), _const_spec(e_wide.shape),
            _const_spec(e_chan.shape),
        ],
        out_specs=rtile(d_in),
        out_shape=jax.ShapeDtypeStruct((b, t, d_in), BF16),
        scratch_shapes=[pltpu.VMEM((SSM_STATE, d_in), F32)],
        compiler_params=_params("parallel", "arbitrary"),
        name="ssd_bwd",
    )(xact, dt, z, yf, alog, dtb, dsk, ng, tri3, e_wide, e_chan)


def _attn_kernel(nct, nsub, q_ref, k_ref, v_ref, o_ref, vp_ref):
    j = pl.program_id(2)
    nkeys = k_ref.shape[1]
    nsteps = nkeys // ROW_TILE // nsub
    first = lax.broadcasted_iota(jnp.int32, (ROW_TILE, LANES), 1) < MLA_V
    one_lane = (MLA_V, 0)

    @pl.when(j == 0)
    def _():
        v = v_ref[0]
        lane = lax.broadcasted_iota(jnp.int32, v.shape, 1)
        for hh in range(2):
            own = (lane < MLA_V) if hh == 0 else (lane >= MLA_V)
            fill = jnp.where(lane == one_lane[hh], 1.0, 0.0).astype(BF16)
            vp_ref[hh] = jnp.where(own, v, fill)

    def attend(sub, nk):
        rs = slice(sub * ROW_TILE, (sub + 1) * ROW_TILE)
        outs = []
        for hh in range(2):
            sl = slice(hh * LANES, (hh + 1) * LANES)
            s = _dot_nt(q_ref[0, rs, sl], k_ref[0, 0:nk, sl])
            p = jnp.exp2(s - jnp.max(s, axis=-1, keepdims=True))
            r = _dot(p.astype(BF16), vp_ref[hh, 0:nk, :])
            outs.append(r / r[:, one_lane[hh]:one_lane[hh] + 1])
        o_ref[0, rs, :] = jnp.where(first, outs[0], outs[1]).astype(BF16)

    def keys_of(tile):
        return nct * ROW_TILE if tile < nct else nkeys

    variants = {}
    for step in range(nsteps):
        variants.setdefault(tuple(keys_of(step * nsub + sub) for sub in range(nsub)), []).append(step)
    for keys, steps in variants.items():
        cond = functools.reduce(jnp.logical_or, [j == st for st in steps])

        def branch(keys=keys):
            for sub, nk in enumerate(keys):
                attend(sub, nk)

        pl.when(cond)(branch)


def _attention(q, k, v, nct):
    b, t, _ = q.shape
    nt = t // ROW_TILE
    pairs = MLA_HEADS // 2
    nsub = next(n for n in ATTN_TILES_PER_STEP if nt % n == 0)
    rows = nsub * ROW_TILE
    return pl.pallas_call(
        functools.partial(_attn_kernel, nct, nsub),
        grid=(b, pairs, nt // nsub),
        in_specs=[
            pl.BlockSpec((1, rows, 2 * LANES), lambda bi, p, j: (bi, j, p)),
            pl.BlockSpec((1, t, 2 * LANES), lambda bi, p, j: (bi, 0, p)),
            pl.BlockSpec((1, t, LANES), lambda bi, p, j: (bi, 0, p)),
        ],
        out_specs=pl.BlockSpec((1, rows, LANES), lambda bi, p, j: (bi, j, p)),
        out_shape=jax.ShapeDtypeStruct((b, t, pairs * LANES), BF16),
        scratch_shapes=[pltpu.VMEM((2, t, LANES), BF16)],
        compiler_params=_params("parallel", "arbitrary", "arbitrary"),
        name="mla_attn",
    )(q, k, v)


def _fmix_kernel(nct, uv_ref, cl_ref, sl_ref, cc_ref, sc_ref, o_ref):
    half = uv_ref.shape[2] // 2
    nctx = nct * ROW_TILE
    nlat = uv_ref.shape[1] - nctx
    for r in range(0, nctx, ROW_TILE):
        o_ref[0, r:r + ROW_TILE, :] = (
            _dot(cc_ref[r:r + ROW_TILE, :], uv_ref[0, 0:nctx, 0:half])
            + _dot(sc_ref[r:r + ROW_TILE, :], uv_ref[0, 0:nctx, half:])).astype(BF16)
    for r in range(0, nlat, ROW_TILE):
        o_ref[0, nctx + r:nctx + r + ROW_TILE, :] = (
            _dot(cl_ref[r:r + ROW_TILE, :], uv_ref[0, nctx:, 0:half])
            + _dot(sl_ref[r:r + ROW_TILE, :], uv_ref[0, nctx:, half:])).astype(BF16)


def _fmix(uv, cl, sl, cc, sc, nct):
    b, t, w = uv.shape
    return pl.pallas_call(
        functools.partial(_fmix_kernel, nct),
        grid=(b,),
        in_specs=[
            pl.BlockSpec((1, t, w), lambda bi: (bi, 0, 0)),
            _const_spec(cl.shape), _const_spec(sl.shape), _const_spec(cc.shape), _const_spec(sc.shape),
        ],
        out_specs=pl.BlockSpec((1, t, w // 2), lambda bi: (bi, 0, 0)),
        out_shape=jax.ShapeDtypeStruct((b, t, w // 2), BF16),
        compiler_params=_params("parallel"),
        name="fourier_mix",
    )(uv, cl, sl, cc, sc)


def _post_kernel(nct, nt, last, x_ref, mod_ref, ys_ref, sc_ref, top_ref, bot_ref, at_ref, fo_ref,
                 gt_ref, scw_ref, wb_ref, wo_ref, g2_ref, w1_ref, w2_ref, fg_ref, o_ref):
    j = pl.program_id(1)

    def body():
        mod = mod_ref[0]
        w = sc_ref.shape[2] // 3
        top_ok = jnp.logical_and(j != 0, j != nct)
        bot_ok = jnp.logical_and(j != nct - 1, j != nt - 1)

        def cx(ref):
            return ref[0, :, w:2 * w].astype(F32) * ref[0, :, 2 * w:3 * w].astype(F32)

        conv = _halo_conv(cx(top_ref), cx(sc_ref), cx(bot_ref), top_ok, bot_ok, scw_ref[...], SC_CONV)
        branches = [ys_ref[0], (sc_ref[0, :, 0:w].astype(F32) * conv).astype(BF16), at_ref[0],
                    fo_ref[0]]
        d = x_ref.shape[2]
        merged = None
        for k in range(N_BRANCHES):
            term = gt_ref[0, :, k * d:(k + 1) * d].astype(F32) * _dot(branches[k], wb_ref[k])
            merged = term if merged is None else merged + term
        x1 = x_ref[0] + mod[2:3] * _dot(merged.astype(BF16), wo_ref[...])
        h2 = (_rms(x1, g2_ref[...]) * (1.0 + mod[4:5]) + mod[3:4]).astype(BF16)
        hid = jnp.maximum(_dot(h2, w1_ref[...]), 0.0)
        x2 = x1 + mod[5:6] * _dot((hid * hid).astype(BF16), w2_ref[...])
        o_ref[0] = _rms(x2, fg_ref[...]) if last else x2

    if last:
        pl.when(j >= nct)(body)
    else:
        body()


def _post(x, mod, ys, sc, at, fo, gt, scw, wb, wo, g2, w1, w2, fg, nct, last):
    b, t, d = x.shape
    nt = t // ROW_TILE
    nb = mod.shape[0] - 1
    hb = ROW_TILE // HALO
    nh = t // HALO
    tile = lambda width: pl.BlockSpec((1, ROW_TILE, width), lambda bi, j: (bi, j, 0))
    scw_w = sc.shape[2]
    if last:
        out_spec = pl.BlockSpec((1, ROW_TILE, d), lambda bi, j: (bi, jnp.maximum(j - nct, 0), 0))
        out_shape = jax.ShapeDtypeStruct((b, t - nct * ROW_TILE, d), F32)
        alias = {}
    else:
        out_spec, out_shape, alias = tile(d), jax.ShapeDtypeStruct((b, t, d), F32), {0: 0}
    return pl.pallas_call(
        functools.partial(_post_kernel, nct, nt, last),
        grid=(b, nt),
        in_specs=[
            tile(d),
            pl.BlockSpec((1, 6, d), lambda bi, j: (jnp.where(j < nct, nb, bi), 0, 0)),
            tile(ys.shape[2]), tile(scw_w),
            pl.BlockSpec((1, HALO, scw_w), lambda bi, j: (bi, jnp.maximum(j * hb - 1, 0), 0)),
            pl.BlockSpec((1, HALO, scw_w), lambda bi, j: (bi, jnp.minimum((j + 1) * hb, nh - 1), 0)),
            tile(at.shape[2]), tile(fo.shape[2]), tile(gt.shape[2]),
            _const_spec(scw.shape), _const_spec(wb.shape), _const_spec(wo.shape),
            _const_spec(g2.shape), _const_spec(w1.shape), _const_spec(w2.shape),
            _const_spec(fg.shape),
        ],
        out_specs=out_spec,
        out_shape=out_shape,
        input_output_aliases=alias,
        compiler_params=_params("parallel", "arbitrary"),
        name="post_merge_mlp",
    )(x, mod, ys, sc, sc, sc, at, fo, gt, scw, wb, wo, g2, w1, w2, fg)


def _rope_table(seq, ctx):
    nf = MLA_ROPE // 4
    pos = jnp.arange(seq)
    inv = ROPE_THETA ** (-jnp.arange(nf, dtype=F32) / nf)
    ang = jnp.stack([(pos // GRID_W).astype(F32)[:, None] * inv,
                     (pos % GRID_W).astype(F32)[:, None] * inv], axis=1)
    cos, sin = jnp.cos(ang), jnp.sin(ang)
    zero = jnp.zeros_like(sin)
    cos_l = jnp.stack([cos, cos], axis=2).reshape(seq, MLA_ROPE)
    sa_l = jnp.stack([-sin, zero], axis=2).reshape(seq, MLA_ROPE)
    sb_l = jnp.stack([zero, sin], axis=2).reshape(seq, MLA_ROPE)

    def lanes(v, fill):
        full = jnp.full((ctx + seq, LANES), fill, F32)
        return full.at[ctx:, _KR_LANE:_KR_LANE + MLA_ROPE].set(v)

    return jnp.concatenate([lanes(cos_l, 1.0), lanes(sa_l, 0.0), lanes(sb_l, 0.0)], axis=1)


def _dft_mats(n):
    idx = jnp.arange(n, dtype=jnp.int32)
    ang = ((idx[:, None] * idx[None, :]) % n).astype(F32) * (2.0 * math.pi / n)
    s = n ** -0.5
    return (jnp.cos(ang) * s).astype(BF16), (-jnp.sin(ang) * s).astype(BF16)


def _channel_dft(width):
    gd = width // FNET_GROUPS
    idx = np.arange(gd)
    ang = 2.0 * np.pi * ((idx[:, None] * idx[None, :]) % gd) / gd
    eye = np.eye(FNET_GROUPS)
    c = np.kron(eye, np.cos(ang)) / math.sqrt(gd)
    s = np.kron(eye, np.sin(ang)) / math.sqrt(gd)
    return jnp.asarray(np.concatenate([c, s], axis=1), BF16)


def _head_pad(w, width, used):
    k = w.shape[0]
    w = w.reshape(k, MLA_HEADS, width)[:, :, :used]
    return jnp.pad(w, ((0, 0), (0, 0), (0, LANES - used))).reshape(k, MLA_HEADS * LANES)


def kernel(x, c, ctx, c_ctx, ada_w, ada_b, norm1_g, norm2_g, w_in, ssm_conv_w, ssm_conv_b, ssm_a_log, ssm_dt_bias, ssm_d, ssm_norm_g, sc_conv_w, mla_qa_g, mla_wqb, mla_kva_g, mla_wkvb, w_branch, w_out, mlp_w1, mlp_w2, final_norm_g):
    b, seq, d = x.shape
    nctx = ctx.shape[1]
    depth = ada_w.shape[0]
    assert seq % ROW_TILE == 0 and nctx % ROW_TILE == 0 and seq % GRID_W == 0
    nct = nctx // ROW_TILE
    d_in = SSM_HEADS * SSM_HEAD_DIM

    xcat = jnp.concatenate([ctx, x], axis=1)
    mods = _modulation(jnp.concatenate([c, c_ctx[None]], axis=0), ada_w, ada_b)

    tab = _rope_table(seq, nctx)
    cl, sl = _dft_mats(seq)
    cc, sc_m = _dft_mats(nctx)
    wc = _channel_dft(d // 2)
    tri_np = np.tril(np.ones((SSM_CHUNK, SSM_CHUNK), np.float32))
    tri3 = jnp.asarray(np.stack([np.tile(tri_np, (1, 3)), np.tile(tri_np.T, (1, 3))]), BF16)
    ew_np = np.zeros((2, LANES, SSM_HEADS * LANES), np.float32)
    ec_np = np.zeros((2, LANES, d_in), np.float32)
    for dr in range(2):
        for hd in range(SSM_HEADS):
            ew_np[dr, SSM_HEADS * dr + hd, hd * LANES:(hd + 1) * LANES] = 1.0
            ec_np[dr, SSM_HEADS * dr + hd, hd * SSM_HEAD_DIM:(hd + 1) * SSM_HEAD_DIM] = 1.0
    e_wide = jnp.asarray(np.tile(ew_np, (1, 3, 1)), BF16)
    e_chan = jnp.asarray(np.tile(ec_np, (1, 3, 1)), BF16)

    def lane_pad(v):
        return jnp.pad(v.reshape(1, -1), ((0, 0), (0, LANES - v.size)))

    for i in range(depth):
        wz, wxbc, wdt, wsc, wqa, wkva, wkr, wfn, wgt = jnp.split(
            w_in[i], np.cumsum([512, 1024, 16, 1536, 384, 256, 32, 512])[:].tolist(), axis=1)
        small = jnp.zeros((d, LANES), F32)
        small = small.at[:, _DT_LANE:_DT_LANE + 16].set(wdt).at[:, _KR_LANE:_KR_LANE + MLA_ROPE].set(wkr)
        w1 = jnp.concatenate([wz, wxbc, wsc, wqa, wkva, wfn, wgt, small], axis=1).astype(BF16)
        wqb = _head_pad(mla_wqb[i], MLA_NOPE + MLA_ROPE, MLA_NOPE + MLA_ROPE).astype(BF16)
        wkk = _head_pad(mla_wkvb[i], MLA_NOPE + MLA_V, MLA_NOPE).astype(BF16)
        wkv = mla_wkvb[i].reshape(-1, MLA_HEADS, MLA_NOPE + MLA_V)[:, :, MLA_NOPE:]
        wkv = wkv.reshape(-1, MLA_HEADS * MLA_V).astype(BF16)

        z, xbc, sc, gt, dt, q, k, v, uv = _pre(
            xcat, mods[i], norm1_g[i][None], w1, mla_qa_g[i][None], wqb, mla_kva_g[i][None], wkk, wkv,
            wc, tab, nct)
        ys = _ssd(xbc, dt, z, ssm_conv_w[i], ssm_conv_b[i][None], lane_pad(ssm_a_log[i]),
                  lane_pad(ssm_dt_bias[i]), jnp.repeat(ssm_d[i], SSM_HEAD_DIM)[None],
                  ssm_norm_g[i][None], tri3, e_wide, e_chan, nct)
        at = _attention(q, k, v, nct)
        fo = _fmix(uv, cl, sl, cc, sc_m, nct)
        xcat = _post(xcat, mods[i], ys, sc, at, fo, gt, sc_conv_w[i], w_branch[i].astype(BF16),
                     w_out[i].astype(BF16), norm2_g[i][None], mlp_w1[i].astype(BF16),
                     mlp_w2[i].astype(BF16), final_norm_g[None], nct, i == depth - 1)
    return xcat
```

```python
import functools
import math

import jax
import jax.numpy as jnp
import numpy as np
from jax import lax
from jax.experimental import pallas as pl
from jax.experimental.pallas import tpu as pltpu

F32 = jnp.float32
BF16 = jnp.bfloat16

LANES = 128
BF16_SUBLANES = 16
VMEM_LIMIT_BYTES = 56 * 1024 * 1024

EPS = 1e-6
ROPE_THETA = 10000.0
GRID_W = 64
N_BRANCHES = 4
SSM_HEADS = 8
SSM_HEAD_DIM = 64
SSM_GROUPS = 2
SSM_HPG = SSM_HEADS // SSM_GROUPS
SSM_STATE = 128
SSM_CHUNK = 128
SSM_CONV = 5
SC_CONV = 3
MLA_HEADS = 8
MLA_NOPE = 64
MLA_ROPE = 32
MLA_V = 64
FNET_GROUPS = 4

ROW_TILE = 256
HALO = BF16_SUBLANES
ATTN_TILES_PER_STEP = (9, 3, 2, 1)


def _const_spec(shape):
    nd = len(shape)
    return pl.BlockSpec(shape, lambda *_: (0,) * nd, pipeline_mode=pl.Buffered(1))


def _params(*sem, flags=None):
    return pltpu.CompilerParams(dimension_semantics=sem, vmem_limit_bytes=VMEM_LIMIT_BYTES, flags=flags)


def _sigmoid(v):
    return 1.0 / (1.0 + jnp.exp(-v))


def _silu(v):
    return v * _sigmoid(v)


def _softplus(v):
    return jnp.maximum(v, 0.0) + jnp.log1p(jnp.exp(-jnp.abs(v)))


def _rms(v, g):
    return v * lax.rsqrt(jnp.mean(v * v, axis=-1, keepdims=True) + EPS) * g


def _dot(a, b):
    return jnp.dot(a, b, preferred_element_type=F32)


def _dot_exact(a, b):
    return jnp.dot(a, b, preferred_element_type=F32, precision=lax.Precision.HIGHEST)


def _dot_nt(a, b):
    return lax.dot_general(a, b, (((1,), (1,)), ((), ())), preferred_element_type=F32)


def _mod_kernel(c_ref, w_ref, b_ref, o_ref):
    o_ref[0, 0] = _dot_exact(_silu(c_ref[...]), w_ref[0]) + b_ref[0, 0]


def _modulation(cc, ada_w, ada_b):
    depth, d, d6 = ada_w.shape
    n = cc.shape[0]
    out = pl.pallas_call(
        _mod_kernel,
        grid=(depth, d6 // d),
        in_specs=[
            pl.BlockSpec((n, d), lambda i, k: (0, 0)),
            pl.BlockSpec((1, d, d), lambda i, k: (i, 0, k)),
            pl.BlockSpec((1, 1, 1, d), lambda i, k: (i, k, 0, 0)),
        ],
        out_specs=pl.BlockSpec((1, 1, n, d), lambda i, k: (i, k, 0, 0)),
        out_shape=jax.ShapeDtypeStruct((depth, d6 // d, n, d), F32),
        compiler_params=_params("arbitrary", "arbitrary"),
        name="adaln_mod",
    )(cc, ada_w, ada_b.reshape(depth, d6 // d, 1, d))
    return jnp.transpose(out, (0, 2, 1, 3))


_C_Z, _C_XBC, _C_SC, _C_QA, _C_KVA, _C_FN, _C_GT, _C_SM, _C_END = (
    0, 512, 1536, 3072, 3456, 3712, 4224, 8320, 8448)
_DT_LANE = 0
_KR_LANE = MLA_NOPE


def _rope(v, tab):
    tc = tab[:, 0:LANES]
    tsa = tab[:, LANES:2 * LANES]
    tsb = tab[:, 2 * LANES:3 * LANES]
    half = MLA_ROPE // 4
    return v * tc + pltpu.roll(v, LANES - half, 1) * tsa + pltpu.roll(v, half, 1) * tsb


def _pre_kernel(x_ref, mod_ref, g1_ref, w_ref, qag_ref, wqb_ref, kvag_ref, wkk_ref, wkv_ref,
                wc_ref, tab_ref,
                z_ref, xbc_ref, sc_ref, gt_ref, dt_ref, q_ref, k_ref, v_ref, uv_ref):
    x = x_ref[0]
    mod = mod_ref[0]
    h = _rms(x, g1_ref[...]) * (1.0 + mod[1:2]) + mod[0:1]
    hb = h.astype(BF16)

    def proj(lo, hi):
        return _dot(hb, w_ref[:, lo:hi])

    z_ref[0] = proj(_C_Z, _C_XBC).astype(BF16)
    xbc_ref[0] = proj(_C_XBC, _C_SC).astype(BF16)
    sc_ref[0] = proj(_C_SC, _C_QA).astype(BF16)
    gt_ref[0] = (1.0 / (1.0 + jnp.exp2(proj(_C_GT, _C_SM)))).astype(BF16)
    small = proj(_C_SM, _C_END)
    dt_ref[0] = small
    tab = tab_ref[...]

    qn = _rms(proj(_C_QA, _C_KVA), qag_ref[...]).astype(BF16)
    q = _dot(qn, wqb_ref[...])
    for hd in range(MLA_HEADS):
        sl = slice(hd * LANES, (hd + 1) * LANES)
        q_ref[0, :, sl] = _rope(q[:, sl], tab).astype(BF16)

    kvn = _rms(proj(_C_KVA, _C_FN), kvag_ref[...]).astype(BF16)
    kn = _dot(kvn, wkk_ref[...])
    lane = lax.broadcasted_iota(jnp.int32, small.shape, 1)
    kr = _rope(jnp.where(lane >= _KR_LANE, small, 0.0), tab)
    for hd in range(MLA_HEADS):
        sl = slice(hd * LANES, (hd + 1) * LANES)
        k_ref[0, :, sl] = (kn[:, sl] + kr).astype(BF16)
    v_ref[0] = _dot(kvn, wkv_ref[...]).astype(BF16)

    fn = proj(_C_FN, _C_GT).astype(BF16)
    uv_ref[0] = _dot(fn, wc_ref[...]).astype(BF16)


def _pre(x, mod, g1, w1, qag, wqb, kvag, wkk, wkv, wc, tab, nct):
    b, t, d = x.shape
    nt = t // ROW_TILE
    nb = mod.shape[0] - 1

    def row(width, dtype=BF16):
        return (pl.BlockSpec((1, ROW_TILE, width), lambda bi, j: (bi, j, 0)),
                jax.ShapeDtypeStruct((b, t, width), dtype))

    outs = [row(512), row(1024), row(1536), row(4096), row(LANES, F32), row(1024), row(1024),
            row(512), row(1024)]
    return pl.pallas_call(
        _pre_kernel,
        grid=(b, nt),
        in_specs=[
            pl.BlockSpec((1, ROW_TILE, d), lambda bi, j: (bi, j, 0)),
            pl.BlockSpec((1, 6, d), lambda bi, j: (jnp.where(j < nct, nb, bi), 0, 0)),
            _const_spec(g1.shape), _const_spec(w1.shape), _const_spec(qag.shape),
            _const_spec(wqb.shape), _const_spec(kvag.shape), _const_spec(wkk.shape),
            _const_spec(wkv.shape), _const_spec(wc.shape),
            pl.BlockSpec((ROW_TILE, 3 * LANES), lambda bi, j: (j, 0)),
        ],
        out_specs=[o[0] for o in outs],
        out_shape=[o[1] for o in outs],
        compiler_params=_params("parallel", "arbitrary"),
        name="pre_proj",
    )(x, mod, g1, w1, qag, wqb, kvag, wkk, wkv, wc, tab)


def _halo_conv(top, mid, bot, top_ok, bot_ok, w, ksize):
    rows = mid.shape[0]
    full = jnp.concatenate([jnp.where(top_ok, top, 0.0), mid, jnp.where(bot_ok, bot, 0.0)], axis=0)
    n = full.shape[0]
    pad = (ksize - 1) // 2
    acc = None
    for k in range(ksize):
        d = k - pad
        sh = full if d == 0 else pltpu.roll(full, (-d) % n, 0)
        term = sh[HALO:HALO + rows] * w[k:k + 1]
        acc = term if acc is None else acc + term
    return acc


def _split3(v, axis):
    hi = v.astype(BF16)
    r1 = v - hi.astype(F32)
    mid = r1.astype(BF16)
    lo = (r1 - mid.astype(F32)).astype(BF16)
    return jnp.concatenate([hi, mid, lo], axis=axis)


def _ssd_chunk(xs, bm, cm, dts, a_row, tri3, st_ref, direction):
    q = xs.shape[0]
    off = SSM_HEADS * direction
    la = dts * a_row
    cum = _dot(tri3, _split3(la, 0))
    cum_t = cum.T[off:off + SSM_HEADS]
    lane = lax.broadcasted_iota(jnp.int32, (q, LANES), 1)
    first_half = lane < SSM_HEAD_DIM

    def spread(v):
        return [jnp.broadcast_to(v[:, off + hd:off + hd + 1], (q, LANES)) for hd in range(SSM_HEADS)]

    def pairs(blocks):
        return jnp.concatenate([jnp.where(first_half, blocks[2 * p], blocks[2 * p + 1])
                                for p in range(SSM_HEADS // 2)], axis=1)

    cum_blk = spread(cum)
    cum_w = jnp.concatenate(cum_blk, axis=1)
    cum_x = pairs(cum_blk)
    dt_x = pairs(spread(dts))
    last = q - 1 if direction == 0 else 0
    tot_x = cum_x[last:last + 1]
    xdt = xs * dt_x
    xw = (xdt * jnp.exp2(tot_x - cum_x)).astype(BF16)
    xdt_b = xdt.astype(BF16)
    keep = tri3[:, 0:q] > 0
    y_pairs = []
    y_off = []
    for g in range(SSM_GROUPS):
        cg = cm[:, g * SSM_STATE:(g + 1) * SSM_STATE]
        bg = bm[:, g * SSM_STATE:(g + 1) * SSM_STATE]
        cb = _dot_nt(cg, bg)
        yd = []
        for hl in range(SSM_HPG):
            hd = g * SSM_HPG + hl
            ci = cum_w[:, hd * LANES:(hd + 1) * LANES]
            cj = cum_t[hd:hd + 1, :]
            seg = jnp.exp2(jnp.where(keep, ci - cj, -jnp.inf))
            m = (cb * seg).astype(BF16)
            pair = hd // 2
            yd.append(_dot(m, xdt_b[:, pair * LANES:(pair + 1) * LANES]))
        for pr in range(SSM_HPG // 2):
            y_pairs.append(jnp.where(first_half, yd[2 * pr], yd[2 * pr + 1]))
        cols = slice(g * SSM_HPG * SSM_HEAD_DIM, (g + 1) * SSM_HPG * SSM_HEAD_DIM)
        st = st_ref[:, cols]
        y_off.append(_dot(cg, st.astype(BF16)))
        upd = _dot(bg.astype(F32).T.astype(BF16), xw[:, cols])
        st_ref[:, cols] = st * jnp.exp2(tot_x[:, cols]) + upd
    y_diag = jnp.concatenate(y_pairs, axis=1)
    return y_diag + jnp.concatenate(y_off, axis=1) * jnp.exp2(cum_x)


def _ssd_f_kernel(nct, nt, xbc_ref, top_ref, bot_ref, dt_ref, cw_ref, cb_ref, alog_ref, dtb_ref,
                  tri_ref, xact_ref, yf_ref, st_ref):
    j = pl.program_id(1)

    @pl.when(j == 0)
    def _():
        st_ref[...] = jnp.zeros_like(st_ref)

    top_ok = jnp.logical_and(j != 0, j != nct)
    bot_ok = jnp.logical_and(j != nct - 1, j != nt - 1)
    conv = _halo_conv(top_ref[0].astype(F32), xbc_ref[0].astype(F32), bot_ref[0].astype(F32),
                      top_ok, bot_ok, cw_ref[...], SSM_CONV)
    act = _silu(conv + cb_ref[...])
    actb = act.astype(BF16)
    xact_ref[0] = actb
    dts = _softplus(dt_ref[0] + dtb_ref[...])
    a_row = -jnp.exp(alog_ref[...]) * math.log2(math.e)
    d_in = SSM_HEADS * SSM_HEAD_DIM
    for c in range(ROW_TILE // SSM_CHUNK):
        r = slice(c * SSM_CHUNK, (c + 1) * SSM_CHUNK)
        yf_ref[0, r, :] = _ssd_chunk(act[r, 0:d_in], actb[r, d_in:d_in + 256], actb[r, d_in + 256:],
                                     dts[r], a_row, tri_ref[0], st_ref, 0)


def _ssd_b_kernel(xact_ref, dt_ref, z_ref, yf_ref, alog_ref, dtb_ref, dsk_ref, ng_ref,
                  tri_ref, y_ref, st_ref):
    j = pl.program_id(1)

    @pl.when(j == 0)
    def _():
        st_ref[...] = jnp.zeros_like(st_ref)

    actb = xact_ref[0]
    dts = _softplus(dt_ref[0] + dtb_ref[...])
    a_row = -jnp.exp(alog_ref[...]) * math.log2(math.e)
    d_in = SSM_HEADS * SSM_HEAD_DIM
    half = d_in // SSM_GROUPS
    for c in reversed(range(ROW_TILE // SSM_CHUNK)):
        r = slice(c * SSM_CHUNK, (c + 1) * SSM_CHUNK)
        xs = actb[r, 0:d_in].astype(F32)
        yb = _ssd_chunk(xs, actb[r, d_in:d_in + 256], actb[r, d_in + 256:], dts[r], a_row,
                        tri_ref[1], st_ref, 1)
        y = (yf_ref[0, r, :] + yb + dsk_ref[...] * xs) * _silu(z_ref[0, r, :].astype(F32))
        for g in range(SSM_GROUPS):
            cols = slice(g * half, (g + 1) * half)
            y_ref[0, r, cols] = _rms(y[:, cols], ng_ref[:, cols]).astype(BF16)


def _ssd(xbc, dt, z, cw, cb, alog, dtb, dsk, ng, tri3, nct):
    b, t, c = xbc.shape
    nt = t // ROW_TILE
    hb = ROW_TILE // HALO
    nh = t // HALO
    d_in = SSM_HEADS * SSM_HEAD_DIM
    tile = lambda width: pl.BlockSpec((1, ROW_TILE, width), lambda bi, j: (bi, j, 0))
    xact, yf = pl.pallas_call(
        functools.partial(_ssd_f_kernel, nct, nt),
        grid=(b, nt),
        in_specs=[
            tile(c),
            pl.BlockSpec((1, HALO, c), lambda bi, j: (bi, jnp.maximum(j * hb - 1, 0), 0)),
            pl.BlockSpec((1, HALO, c), lambda bi, j: (bi, jnp.minimum((j + 1) * hb, nh - 1), 0)),
            tile(LANES),
            _const_spec(cw.shape), _const_spec(cb.shape), _const_spec(alog.shape),
            _const_spec(dtb.shape), _const_spec(tri3.shape),
        ],
        out_specs=[tile(c), tile(d_in)],
        out_shape=[jax.ShapeDtypeStruct((b, t, c), BF16), jax.ShapeDtypeStruct((b, t, d_in), F32)],
        scratch_shapes=[pltpu.VMEM((SSM_STATE, d_in), F32)],
        compiler_params=_params("parallel", "arbitrary"),
        name="ssd_fwd",
    )(xbc, xbc, xbc, dt, cw, cb, alog, dtb, tri3)

    def rev(j):
        return jnp.where(j < nct, nct - 1 - j, nt - 1 - (j - nct))

    rtile = lambda width: pl.BlockSpec((1, ROW_TILE, width), lambda bi, j: (bi, rev(j), 0))
    return pl.pallas_call(
        _ssd_b_kernel,
        grid=(b, nt),
        in_specs=[
            rtile(c), rtile(LANES), rtile(d_in), rtile(d_in),
            _const_spec(alog.shape), _const_spec(dtb.shape), _const_spec(dsk.shape),
            _const_spec(ng.shape), _const_spec(tri3.shape),
        ],
        out_specs=rtile(d_in),
        out_shape=jax.ShapeDtypeStruct((b, t, d_in), BF16),
        scratch_shapes=[pltpu.VMEM((SSM_STATE, d_in), F32)],
        compiler_params=_params("parallel", "arbitrary"),
        name="ssd_bwd",
    )(xact, dt, z, yf, alog, dtb, dsk, ng, tri3)


def _attn_kernel(nct, nsub, q_ref, k_ref, v_ref, o_ref, vp_ref):
    j = pl.program_id(2)
    nkeys = k_ref.shape[1]
    nsteps = nkeys // ROW_TILE // nsub
    first = lax.broadcasted_iota(jnp.int32, (ROW_TILE, LANES), 1) < MLA_V
    one_lane = (MLA_V, 0)

    @pl.when(j == 0)
    def _():
        v = v_ref[0]
        lane = lax.broadcasted_iota(jnp.int32, v.shape, 1)
        for hh in range(2):
            own = (lane < MLA_V) if hh == 0 else (lane >= MLA_V)
            fill = jnp.where(lane == one_lane[hh], 1.0, 0.0).astype(BF16)
            vp_ref[hh] = jnp.where(own, v, fill)

    def attend(sub, nk):
        rs = slice(sub * ROW_TILE, (sub + 1) * ROW_TILE)
        outs = []
        for hh in range(2):
            sl = slice(hh * LANES, (hh + 1) * LANES)
            s = _dot_nt(q_ref[0, rs, sl], k_ref[0, 0:nk, sl])
            p = jnp.exp2(s - jnp.max(s, axis=-1, keepdims=True))
            r = _dot(p.astype(BF16), vp_ref[hh, 0:nk, :])
            outs.append(r / r[:, one_lane[hh]:one_lane[hh] + 1])
        o_ref[0, rs, :] = jnp.where(first, outs[0], outs[1]).astype(BF16)

    def keys_of(tile):
        return nct * ROW_TILE if tile < nct else nkeys

    variants = {}
    for step in range(nsteps):
        variants.setdefault(tuple(keys_of(step * nsub + sub) for sub in range(nsub)), []).append(step)
    for keys, steps in variants.items():
        cond = functools.reduce(jnp.logical_or, [j == st for st in steps])

        def branch(keys=keys):
            for sub, nk in enumerate(keys):
                attend(sub, nk)

        pl.when(cond)(branch)


def _attention(q, k, v, nct):
    b, t, _ = q.shape
    nt = t // ROW_TILE
    pairs = MLA_HEADS // 2
    nsub = next(n for n in ATTN_TILES_PER_STEP if nt % n == 0)
    rows = nsub * ROW_TILE
    return pl.pallas_call(
        functools.partial(_attn_kernel, nct, nsub),
        grid=(b, pairs, nt // nsub),
        in_specs=[
            pl.BlockSpec((1, rows, 2 * LANES), lambda bi, p, j: (bi, j, p)),
            pl.BlockSpec((1, t, 2 * LANES), lambda bi, p, j: (bi, 0, p)),
            pl.BlockSpec((1, t, LANES), lambda bi, p, j: (bi, 0, p)),
        ],
        out_specs=pl.BlockSpec((1, rows, LANES), lambda bi, p, j: (bi, j, p)),
        out_shape=jax.ShapeDtypeStruct((b, t, pairs * LANES), BF16),
        scratch_shapes=[pltpu.VMEM((2, t, LANES), BF16)],
        compiler_params=_params("parallel", "arbitrary", "arbitrary"),
        name="mla_attn",
    )(q, k, v)


def _fmix_kernel(nct, uv_ref, cl_ref, sl_ref, cc_ref, sc_ref, o_ref):
    half = uv_ref.shape[2] // 2
    nctx = nct * ROW_TILE
    nlat = uv_ref.shape[1] - nctx
    for r in range(0, nctx, ROW_TILE):
        o_ref[0, r:r + ROW_TILE, :] = (
            _dot(cc_ref[r:r + ROW_TILE, :], uv_ref[0, 0:nctx, 0:half])
            + _dot(sc_ref[r:r + ROW_TILE, :], uv_ref[0, 0:nctx, half:])).astype(BF16)
    for r in range(0, nlat, ROW_TILE):
        o_ref[0, nctx + r:nctx + r + ROW_TILE, :] = (
            _dot(cl_ref[r:r + ROW_TILE, :], uv_ref[0, nctx:, 0:half])
            + _dot(sl_ref[r:r + ROW_TILE, :], uv_ref[0, nctx:, half:])).astype(BF16)


def _fmix(uv, cl, sl, cc, sc, nct):
    b, t, w = uv.shape
    return pl.pallas_call(
        functools.partial(_fmix_kernel, nct),
        grid=(b,),
        in_specs=[
            pl.BlockSpec((1, t, w), lambda bi: (bi, 0, 0)),
            _const_spec(cl.shape), _const_spec(sl.shape), _const_spec(cc.shape), _const_spec(sc.shape),
        ],
        out_specs=pl.BlockSpec((1, t, w // 2), lambda bi: (bi, 0, 0)),
        out_shape=jax.ShapeDtypeStruct((b, t, w // 2), BF16),
        compiler_params=_params("parallel"),
        name="fourier_mix",
    )(uv, cl, sl, cc, sc)


def _post_kernel(nct, nt, last, x_ref, mod_ref, ys_ref, sc_ref, top_ref, bot_ref, at_ref, fo_ref,
                 gt_ref, scw_ref, wb_ref, wo_ref, g2_ref, w1_ref, w2_ref, fg_ref, o_ref):
    j = pl.program_id(1)

    def body():
        mod = mod_ref[0]
        w = sc_ref.shape[2] // 3
        top_ok = jnp.logical_and(j != 0, j != nct)
        bot_ok = jnp.logical_and(j != nct - 1, j != nt - 1)

        def cx(ref):
            return ref[0, :, w:2 * w].astype(F32) * ref[0, :, 2 * w:3 * w].astype(F32)

        conv = _halo_conv(cx(top_ref), cx(sc_ref), cx(bot_ref), top_ok, bot_ok, scw_ref[...], SC_CONV)
        branches = [ys_ref[0], (sc_ref[0, :, 0:w].astype(F32) * conv).astype(BF16), at_ref[0],
                    fo_ref[0]]
        d = x_ref.shape[2]
        merged = None
        for k in range(N_BRANCHES):
            term = gt_ref[0, :, k * d:(k + 1) * d].astype(F32) * _dot(branches[k], wb_ref[k])
            merged = term if merged is None else merged + term
        x1 = x_ref[0] + mod[2:3] * _dot(merged.astype(BF16), wo_ref[...])
        h2 = (_rms(x1, g2_ref[...]) * (1.0 + mod[4:5]) + mod[3:4]).astype(BF16)
        hid = jnp.maximum(_dot(h2, w1_ref[...]), 0.0)
        x2 = x1 + mod[5:6] * _dot((hid * hid).astype(BF16), w2_ref[...])
        o_ref[0] = _rms(x2, fg_ref[...]) if last else x2

    if last:
        pl.when(j >= nct)(body)
    else:
        body()


def _post(x, mod, ys, sc, at, fo, gt, scw, wb, wo, g2, w1, w2, fg, nct, last):
    b, t, d = x.shape
    nt = t // ROW_TILE
    nb = mod.shape[0] - 1
    hb = ROW_TILE // HALO
    nh = t // HALO
    tile = lambda width: pl.BlockSpec((1, ROW_TILE, width), lambda bi, j: (bi, j, 0))
    scw_w = sc.shape[2]
    if last:
        out_spec = pl.BlockSpec((1, ROW_TILE, d), lambda bi, j: (bi, jnp.maximum(j - nct, 0), 0))
        out_shape = jax.ShapeDtypeStruct((b, t - nct * ROW_TILE, d), F32)
        alias = {}
    else:
        out_spec, out_shape, alias = tile(d), jax.ShapeDtypeStruct((b, t, d), F32), {0: 0}
    return pl.pallas_call(
        functools.partial(_post_kernel, nct, nt, last),
        grid=(b, nt),
        in_specs=[
            tile(d),
            pl.BlockSpec((1, 6, d), lambda bi, j: (jnp.where(j < nct, nb, bi), 0, 0)),
            tile(ys.shape[2]), tile(scw_w),
            pl.BlockSpec((1, HALO, scw_w), lambda bi, j: (bi, jnp.maximum(j * hb - 1, 0), 0)),
            pl.BlockSpec((1, HALO, scw_w), lambda bi, j: (bi, jnp.minimum((j + 1) * hb, nh - 1), 0)),
            tile(at.shape[2]), tile(fo.shape[2]), tile(gt.shape[2]),
            _const_spec(scw.shape), _const_spec(wb.shape), _const_spec(wo.shape),
            _const_spec(g2.shape), _const_spec(w1.shape), _const_spec(w2.shape),
            _const_spec(fg.shape),
        ],
        out_specs=out_spec,
        out_shape=out_shape,
        input_output_aliases=alias,
        compiler_params=_params("parallel", "arbitrary"),
        name="post_merge_mlp",
    )(x, mod, ys, sc, sc, sc, at, fo, gt, scw, wb, wo, g2, w1, w2, fg)


def _rope_table(seq, ctx):
    nf = MLA_ROPE // 4
    pos = jnp.arange(seq)
    inv = ROPE_THETA ** (-jnp.arange(nf, dtype=F32) / nf)
    ang = jnp.stack([(pos // GRID_W).astype(F32)[:, None] * inv,
                     (pos % GRID_W).astype(F32)[:, None] * inv], axis=1)
    cos, sin = jnp.cos(ang), jnp.sin(ang)
    zero = jnp.zeros_like(sin)
    cos_l = jnp.stack([cos, cos], axis=2).reshape(seq, MLA_ROPE)
    sa_l = jnp.stack([-sin, zero], axis=2).reshape(seq, MLA_ROPE)
    sb_l = jnp.stack([zero, sin], axis=2).reshape(seq, MLA_ROPE)

    def lanes(v, fill):
        full = jnp.full((ctx + seq, LANES), fill, F32)
        return full.at[ctx:, _KR_LANE:_KR_LANE + MLA_ROPE].set(v)

    return jnp.concatenate([lanes(cos_l, 1.0), lanes(sa_l, 0.0), lanes(sb_l, 0.0)], axis=1)


def _dft_mats(n):
    w = math.gcd(n, 64)
    k = jnp.arange(n, dtype=jnp.int32)[:, None]

    def table(cols):
        ang = ((k * cols[None, :]) % n).astype(F32) * (2.0 * math.pi / n)
        return jnp.cos(ang), jnp.sin(ang)

    ca, sa = table(jnp.arange(n // w, dtype=jnp.int32) * w)
    cb, sb = table(jnp.arange(w, dtype=jnp.int32))
    s = n ** -0.5
    cos = ca[:, :, None] * cb[:, None, :] - sa[:, :, None] * sb[:, None, :]
    sin = sa[:, :, None] * cb[:, None, :] + ca[:, :, None] * sb[:, None, :]
    return (cos * s).reshape(n, n).astype(BF16), (-sin * s).reshape(n, n).astype(BF16)


def _channel_dft(width):
    gd = width // FNET_GROUPS
    idx = np.arange(gd)
    ang = 2.0 * np.pi * ((idx[:, None] * idx[None, :]) % gd) / gd
    eye = np.eye(FNET_GROUPS)
    c = np.kron(eye, np.cos(ang)) / math.sqrt(gd)
    s = np.kron(eye, np.sin(ang)) / math.sqrt(gd)
    return jnp.asarray(np.concatenate([c, s], axis=1), BF16)


def _head_pad(w, width, used):
    k = w.shape[0]
    w = w.reshape(k, MLA_HEADS, width)[:, :, :used]
    return jnp.pad(w, ((0, 0), (0, 0), (0, LANES - used))).reshape(k, MLA_HEADS * LANES)


def kernel(x, c, ctx, c_ctx, ada_w, ada_b, norm1_g, norm2_g, w_in, ssm_conv_w, ssm_conv_b, ssm_a_log, ssm_dt_bias, ssm_d, ssm_norm_g, sc_conv_w, mla_qa_g, mla_wqb, mla_kva_g, mla_wkvb, w_branch, w_out, mlp_w1, mlp_w2, final_norm_g):
    b, seq, d = x.shape
    nctx = ctx.shape[1]
    depth = ada_w.shape[0]
    assert seq % ROW_TILE == 0 and nctx % ROW_TILE == 0 and seq % GRID_W == 0
    nct = nctx // ROW_TILE
    d_in = SSM_HEADS * SSM_HEAD_DIM

    xcat = jnp.concatenate([ctx, x], axis=1)
    mods = _modulation(jnp.concatenate([c, c_ctx[None]], axis=0), ada_w, ada_b)

    tab = _rope_table(seq, nctx)
    cl, sl = _dft_mats(seq)
    cc, sc_m = _dft_mats(nctx)
    wc = _channel_dft(d // 2)
    tri_np = np.tril(np.ones((SSM_CHUNK, SSM_CHUNK), np.float32))
    tri3 = jnp.asarray(np.stack([np.tile(tri_np, (1, 3)), np.tile(tri_np.T, (1, 3))]), BF16)

    def lane_pad(v):
        return jnp.pad(v.reshape(1, -1), ((0, 0), (0, LANES - v.size)))

    for i in range(depth):
        wz, wxbc, wdt, wsc, wqa, wkva, wkr, wfn, wgt = jnp.split(
            w_in[i], np.cumsum([512, 1024, 16, 1536, 384, 256, 32, 512])[:].tolist(), axis=1)
        small = jnp.zeros((d, LANES), F32)
        small = small.at[:, _DT_LANE:_DT_LANE + 16].set(wdt).at[:, _KR_LANE:_KR_LANE + MLA_ROPE].set(wkr)
        w1 = jnp.concatenate([wz, wxbc, wsc, wqa, wkva, wfn, wgt * -math.log2(math.e), small],
                             axis=1).astype(BF16)
        q_scale = (MLA_NOPE + MLA_ROPE) ** -0.5 * math.log2(math.e)
        wqb = _head_pad(mla_wqb[i] * q_scale, MLA_NOPE + MLA_ROPE, MLA_NOPE + MLA_ROPE).astype(BF16)
        wkk = _head_pad(mla_wkvb[i], MLA_NOPE + MLA_V, MLA_NOPE).astype(BF16)
        wkv = mla_wkvb[i].reshape(-1, MLA_HEADS, MLA_NOPE + MLA_V)[:, :, MLA_NOPE:]
        wkv = wkv.reshape(-1, MLA_HEADS * MLA_V).astype(BF16)

        z, xbc, sc, gt, dt, q, k, v, uv = _pre(
            xcat, mods[i], norm1_g[i][None], w1, mla_qa_g[i][None], wqb, mla_kva_g[i][None], wkk, wkv,
            wc, tab, nct)
        ys = _ssd(xbc, dt, z, ssm_conv_w[i], ssm_conv_b[i][None], lane_pad(ssm_a_log[i]),
                  lane_pad(ssm_dt_bias[i]), jnp.repeat(ssm_d[i], SSM_HEAD_DIM)[None],
                  ssm_norm_g[i][None], tri3, nct)
        at = _attention(q, k, v, nct)
        fo = _fmix(uv, cl, sl, cc, sc_m, nct)
        xcat = _post(xcat, mods[i], ys, sc, at, fo, gt, sc_conv_w[i], w_branch[i].astype(BF16),
                     w_out[i].astype(BF16), norm2_g[i][None], mlp_w1[i].astype(BF16),
                     mlp_w2[i].astype(BF16), final_norm_g[None], nct, i == depth - 1)
    return xcat
```

```python
import functools
import math

import jax
import jax.numpy as jnp
import numpy as np
from jax import lax
from jax.experimental import pallas as pl
from jax.experimental.pallas import tpu as pltpu

F32 = jnp.float32
BF16 = jnp.bfloat16

LANES = 128
F32_SUBLANES = 8
BF16_SUBLANES = 16
VMEM_LIMIT_BYTES = 56 * 1024 * 1024

EPS = 1e-6
ROPE_THETA = 10000.0
GRID_W = 64
N_BRANCHES = 4
SSM_HEADS = 8
SSM_HEAD_DIM = 64
SSM_GROUPS = 2
SSM_HPG = SSM_HEADS // SSM_GROUPS
SSM_STATE = 128
SSM_CHUNK = 128
SSM_CONV = 5
SC_CONV = 3
MLA_HEADS = 8
MLA_NOPE = 64
MLA_ROPE = 32
MLA_V = 64
FNET_GROUPS = 4

ROW_TILE = 256
HALO = BF16_SUBLANES
ATTN_TILES_PER_STEP = (9, 3, 2, 1)


def _const_spec(shape):
    nd = len(shape)
    return pl.BlockSpec(shape, lambda *_: (0,) * nd, pipeline_mode=pl.Buffered(1))


def _params(*sem, flags=None):
    return pltpu.CompilerParams(dimension_semantics=sem, vmem_limit_bytes=VMEM_LIMIT_BYTES, flags=flags)


def _sigmoid(v):
    return 1.0 / (1.0 + jnp.exp(-v))


def _silu(v):
    return v * _sigmoid(v)


def _softplus(v):
    return jnp.maximum(v, 0.0) + jnp.log1p(jnp.exp(-jnp.abs(v)))


def _rms(v, g):
    return v * lax.rsqrt(jnp.mean(v * v, axis=-1, keepdims=True) + EPS) * g


def _dot(a, b):
    return jnp.dot(a, b, preferred_element_type=F32)


def _dot_exact(a, b):
    return jnp.dot(a, b, preferred_element_type=F32, precision=lax.Precision.HIGHEST)


def _dot_nt(a, b):
    return lax.dot_general(a, b, (((1,), (1,)), ((), ())), preferred_element_type=F32)


def _mod_kernel(c_ref, w_ref, b_ref, o_ref):
    o_ref[0, 0] = _dot_exact(_silu(c_ref[...]), w_ref[0]) + b_ref[0, 0]


def _modulation(cc, ada_w, ada_b):
    depth, d, d6 = ada_w.shape
    n = cc.shape[0]
    out = pl.pallas_call(
        _mod_kernel,
        grid=(depth, d6 // d),
        in_specs=[
            pl.BlockSpec((n, d), lambda i, k: (0, 0)),
            pl.BlockSpec((1, d, d), lambda i, k: (i, 0, k)),
            pl.BlockSpec((1, 1, 1, d), lambda i, k: (i, k, 0, 0)),
        ],
        out_specs=pl.BlockSpec((1, 1, n, d), lambda i, k: (i, k, 0, 0)),
        out_shape=jax.ShapeDtypeStruct((depth, d6 // d, n, d), F32),
        compiler_params=_params("arbitrary", "arbitrary"),
        name="adaln_mod",
    )(cc, ada_w, ada_b.reshape(depth, d6 // d, 1, d))
    return jnp.transpose(out, (0, 2, 1, 3))


_C_Z, _C_XBC, _C_SC, _C_QA, _C_KVA, _C_FN, _C_GT, _C_SM, _C_END = (
    0, 512, 1536, 3072, 3456, 3712, 4224, 8320, 8448)
_DT_LANE = 0
_KR_LANE = MLA_NOPE


def _rope(v, tab):
    tc = tab[:, 0:LANES]
    tsa = tab[:, LANES:2 * LANES]
    tsb = tab[:, 2 * LANES:3 * LANES]
    half = MLA_ROPE // 4
    return v * tc + pltpu.roll(v, LANES - half, 1) * tsa + pltpu.roll(v, half, 1) * tsb


def _pre_kernel(nt, x_ref, mod_ref, g1_ref, w_ref, qag_ref, wqb_ref, kvag_ref, wkk_ref, wkv_ref,
                wc_ref, tab_ref,
                z_ref, xbc_ref, sc_ref, gt_ref, dt_ref, q_ref, k_ref, v_ref, uv_ref,
                qn0_ref, kvn0_ref, fn0_ref, sm0_ref, qn1_ref, kvn1_ref, fn1_ref, sm1_ref):
    j = pl.program_id(1)

    def stage_a(qn_ref, kvn_ref, fn_ref, sm_ref):
        mod = mod_ref[0]
        hb = (_rms(x_ref[0], g1_ref[...]) * (1.0 + mod[1:2]) + mod[0:1]).astype(BF16)

        def proj(lo, hi):
            return _dot(hb, w_ref[:, lo:hi])

        qn_ref[...] = _rms(proj(_C_QA, _C_KVA), qag_ref[...]).astype(BF16)
        kvn_ref[...] = _rms(proj(_C_KVA, _C_FN), kvag_ref[...]).astype(BF16)
        fn_ref[...] = proj(_C_FN, _C_GT).astype(BF16)
        small = proj(_C_SM, _C_END)
        sm_ref[...] = small
        dt_ref[0] = small
        z_ref[0] = proj(_C_Z, _C_XBC).astype(BF16)
        xbc_ref[0] = proj(_C_XBC, _C_SC).astype(BF16)
        sc_ref[0] = proj(_C_SC, _C_QA).astype(BF16)
        gt_ref[0] = (1.0 / (1.0 + jnp.exp2(proj(_C_GT, _C_SM)))).astype(BF16)

    def stage_b(qn_ref, kvn_ref, fn_ref, sm_ref):
        tab = tab_ref[...]
        q = _dot(qn_ref[...], wqb_ref[...])
        for hd in range(MLA_HEADS):
            sl = slice(hd * LANES, (hd + 1) * LANES)
            q_ref[0, :, sl] = _rope(q[:, sl], tab).astype(BF16)
        kvn = kvn_ref[...]
        kn = _dot(kvn, wkk_ref[...])
        small = sm_ref[...]
        lane = lax.broadcasted_iota(jnp.int32, small.shape, 1)
        kr = _rope(jnp.where(lane >= _KR_LANE, small, 0.0), tab)
        for hd in range(MLA_HEADS):
            sl = slice(hd * LANES, (hd + 1) * LANES)
            k_ref[0, :, sl] = (kn[:, sl] + kr).astype(BF16)
        v_ref[0] = _dot(kvn, wkv_ref[...]).astype(BF16)
        uv_ref[0] = _dot(fn_ref[...], wc_ref[...]).astype(BF16)

    bufs = ((qn0_ref, kvn0_ref, fn0_ref, sm0_ref), (qn1_ref, kvn1_ref, fn1_ref, sm1_ref))
    variants = {}
    for step in range(nt + 1):
        variants.setdefault((step < nt, step >= 1, step % 2), []).append(step)
    for (do_a, do_b, parity), steps in variants.items():
        cond = functools.reduce(jnp.logical_or, [j == st for st in steps])

        def branch(do_a=do_a, do_b=do_b, parity=parity):
            if do_b:
                stage_b(*bufs[1 - parity])
            if do_a:
                stage_a(*bufs[parity])

        pl.when(cond)(branch)


def _pre(x, mod, g1, w1, qag, wqb, kvag, wkk, wkv, wc, tab, nct):
    b, t, d = x.shape
    nt = t // ROW_TILE
    nb = mod.shape[0] - 1

    def ta(j):
        return jnp.minimum(j, nt - 1)

    def tb(j):
        return jnp.maximum(j - 1, 0)

    def row(width, tile, dtype=BF16):
        return (pl.BlockSpec((1, ROW_TILE, width), lambda bi, j: (bi, tile(j), 0)),
                jax.ShapeDtypeStruct((b, t, width), dtype))

    outs = [row(512, ta), row(1024, ta), row(1536, ta), row(4096, ta), row(LANES, ta, F32),
            row(1024, tb), row(1024, tb), row(512, tb), row(1024, tb)]
    stage_bufs = [pltpu.VMEM((ROW_TILE, _C_KVA - _C_QA), BF16), pltpu.VMEM((ROW_TILE, _C_FN - _C_KVA), BF16),
                  pltpu.VMEM((ROW_TILE, _C_GT - _C_FN), BF16), pltpu.VMEM((ROW_TILE, LANES), F32)]
    return pl.pallas_call(
        functools.partial(_pre_kernel, nt),
        grid=(b, nt + 1),
        in_specs=[
            pl.BlockSpec((1, ROW_TILE, d), lambda bi, j: (bi, ta(j), 0)),
            pl.BlockSpec((1, 6, d), lambda bi, j: (jnp.where(ta(j) < nct, nb, bi), 0, 0)),
            _const_spec(g1.shape), _const_spec(w1.shape), _const_spec(qag.shape),
            _const_spec(wqb.shape), _const_spec(kvag.shape), _const_spec(wkk.shape),
            _const_spec(wkv.shape), _const_spec(wc.shape),
            pl.BlockSpec((ROW_TILE, 3 * LANES), lambda bi, j: (tb(j), 0)),
        ],
        out_specs=[o[0] for o in outs],
        out_shape=[o[1] for o in outs],
        scratch_shapes=stage_bufs + stage_bufs,
        compiler_params=_params("arbitrary", "arbitrary"),
        name="pre_proj",
    )(x, mod, g1, w1, qag, wqb, kvag, wkk, wkv, wc, tab)


def _halo_conv(top, mid, bot, top_ok, bot_ok, w, ksize):
    rows = mid.shape[0]
    full = jnp.concatenate([jnp.where(top_ok, top, 0.0), mid, jnp.where(bot_ok, bot, 0.0)], axis=0)
    n = full.shape[0]
    pad = (ksize - 1) // 2
    acc = None
    for k in range(ksize):
        d = k - pad
        sh = full if d == 0 else pltpu.roll(full, (-d) % n, 0)
        term = sh[HALO:HALO + rows] * w[k:k + 1]
        acc = term if acc is None else acc + term
    return acc


def _split3(v, axis):
    hi = v.astype(BF16)
    r1 = v - hi.astype(F32)
    mid = r1.astype(BF16)
    lo = (r1 - mid.astype(F32)).astype(BF16)
    return jnp.concatenate([hi, mid, lo], axis=axis)


def _ssd_local(xs, bm, cm, dts, a_row, tri3, direction):
    q = xs.shape[0]
    off = SSM_HEADS * direction
    la = dts * a_row
    cum = _dot(tri3, _split3(la, 0))
    cum_t = cum.T[off:off + SSM_HEADS]
    lane = lax.broadcasted_iota(jnp.int32, (q, LANES), 1)
    first_half = lane < SSM_HEAD_DIM

    def spread(v):
        return [jnp.broadcast_to(v[:, off + hd:off + hd + 1], (q, LANES)) for hd in range(SSM_HEADS)]

    def pairs(blocks):
        return jnp.concatenate([jnp.where(first_half, blocks[2 * p], blocks[2 * p + 1])
                                for p in range(SSM_HEADS // 2)], axis=1)

    cum_blk = spread(cum)
    cum_w = jnp.concatenate(cum_blk, axis=1)
    cum_x = pairs(cum_blk)
    dt_x = pairs(spread(dts))
    last = q - 1 if direction == 0 else 0
    tot_x = cum_x[last:last + 1]
    xdt = xs * dt_x
    xw = (xdt * jnp.exp2(tot_x - cum_x)).astype(BF16)
    xdt_b = xdt.astype(BF16)
    keep = tri3[:, 0:q] > 0
    y_pairs = []
    upd = []
    for g in range(SSM_GROUPS):
        cg = cm[:, g * SSM_STATE:(g + 1) * SSM_STATE]
        bg = bm[:, g * SSM_STATE:(g + 1) * SSM_STATE]
        cb = _dot_nt(cg, bg)
        yd = []
        for hl in range(SSM_HPG):
            hd = g * SSM_HPG + hl
            ci = cum_w[:, hd * LANES:(hd + 1) * LANES]
            cj = cum_t[hd:hd + 1, :]
            seg = jnp.exp2(jnp.where(keep, ci - cj, -jnp.inf))
            m = (cb * seg).astype(BF16)
            pair = hd // 2
            yd.append(_dot(m, xdt_b[:, pair * LANES:(pair + 1) * LANES]))
        for pr in range(SSM_HPG // 2):
            y_pairs.append(jnp.where(first_half, yd[2 * pr], yd[2 * pr + 1]))
        cols = slice(g * SSM_HPG * SSM_HEAD_DIM, (g + 1) * SSM_HPG * SSM_HEAD_DIM)
        upd.append(_dot(bg.astype(F32).T.astype(BF16), xw[:, cols]))
    return (jnp.concatenate(y_pairs, axis=1), jnp.exp2(cum_x), jnp.concatenate(upd, axis=1),
            jnp.exp2(tot_x))


def _ssd_carry(y_diag, ecum, cm, upd, dec, st_ref):
    st = st_ref[...]
    stb = st.astype(BF16)
    width = SSM_HPG * SSM_HEAD_DIM
    y_off = jnp.concatenate(
        [_dot(cm[:, g * SSM_STATE:(g + 1) * SSM_STATE], stb[:, g * width:(g + 1) * width])
         for g in range(SSM_GROUPS)], axis=1)
    st_ref[...] = st * dec + upd
    return y_diag + y_off * ecum


class _SsdStage:
    def __init__(self, yd, ec, cm, upd, dec):
        self.yd, self.ec, self.cm, self.upd, self.dec = yd, ec, cm, upd, dec

    @staticmethod
    def shapes(d_in):
        return [pltpu.VMEM((ROW_TILE, d_in), F32), pltpu.VMEM((ROW_TILE, d_in), F32),
                pltpu.VMEM((ROW_TILE, SSM_GROUPS * SSM_STATE), BF16),
                pltpu.VMEM((ROW_TILE // SSM_CHUNK * SSM_STATE, d_in), F32),
                pltpu.VMEM((ROW_TILE // SSM_CHUNK * F32_SUBLANES, d_in), F32)]

    def put(self, c, actb, dts, a_row, tri3, direction):
        d_in = SSM_HEADS * SSM_HEAD_DIM
        nbc = SSM_GROUPS * SSM_STATE
        r = slice(c * SSM_CHUNK, (c + 1) * SSM_CHUNK)
        yd, ec, upd, dec = _ssd_local(actb[r, 0:d_in].astype(F32), actb[r, d_in:d_in + nbc],
                                      actb[r, d_in + nbc:], dts[r], a_row, tri3, direction)
        self.yd[r, :] = yd
        self.ec[r, :] = ec
        self.cm[r, :] = actb[r, d_in + nbc:]
        self.upd[c * SSM_STATE:(c + 1) * SSM_STATE, :] = upd
        self.dec[c * F32_SUBLANES:c * F32_SUBLANES + 1, :] = dec

    def take(self, c, st_ref):
        r = slice(c * SSM_CHUNK, (c + 1) * SSM_CHUNK)
        return _ssd_carry(self.yd[r, :], self.ec[r, :], self.cm[r, :],
                          self.upd[c * SSM_STATE:(c + 1) * SSM_STATE, :],
                          self.dec[c * F32_SUBLANES:c * F32_SUBLANES + 1, :], st_ref)


def _staged(j, nsteps, stage_a, stage_b, bufs):
    variants = {}
    for step in range(nsteps + 1):
        variants.setdefault((step < nsteps, step >= 1, step % 2), []).append(step)
    for (do_a, do_b, parity), steps in variants.items():
        cond = functools.reduce(jnp.logical_or, [j == st for st in steps])

        def branch(do_a=do_a, do_b=do_b, parity=parity):
            if do_b:
                stage_b(bufs[1 - parity])
            if do_a:
                stage_a(bufs[parity])

        pl.when(cond)(branch)


_SSD_CHUNKS = tuple(range(ROW_TILE // SSM_CHUNK))
_N_STAGE_REFS = 5


def _ssd_f_kernel(nct, nt, xbc_ref, top_ref, bot_ref, dt_ref, cw_ref, cb_ref, alog_ref, dtb_ref,
                  tri_ref, xact_ref, yf_ref, st_ref, *stage_refs):
    j = pl.program_id(1)

    @pl.when(j == 0)
    def _():
        st_ref[...] = jnp.zeros_like(st_ref)

    def stage_a(buf):
        top_ok = jnp.logical_and(j != 0, j != nct)
        bot_ok = jnp.logical_and(j != nct - 1, j != nt - 1)
        conv = _halo_conv(top_ref[0].astype(F32), xbc_ref[0].astype(F32), bot_ref[0].astype(F32),
                          top_ok, bot_ok, cw_ref[...], SSM_CONV)
        actb = _silu(conv + cb_ref[...]).astype(BF16)
        xact_ref[0] = actb
        dts = _softplus(dt_ref[0] + dtb_ref[...])
        a_row = -jnp.exp(alog_ref[...]) * math.log2(math.e)
        for c in _SSD_CHUNKS:
            buf.put(c, actb, dts, a_row, tri_ref[0], 0)

    def stage_b(buf):
        for c in _SSD_CHUNKS:
            yf_ref[0, c * SSM_CHUNK:(c + 1) * SSM_CHUNK, :] = buf.take(c, st_ref)

    bufs = (_SsdStage(*stage_refs[:_N_STAGE_REFS]), _SsdStage(*stage_refs[_N_STAGE_REFS:]))
    _staged(j, nt, stage_a, stage_b, bufs)


def _ssd_b_kernel(nt, xact_ref, dt_ref, xs_ref, z_ref, yf_ref, alog_ref, dtb_ref, dsk_ref, ng_ref,
                  tri_ref, y_ref, st_ref, *stage_refs):
    j = pl.program_id(1)

    @pl.when(j == 0)
    def _():
        st_ref[...] = jnp.zeros_like(st_ref)

    def stage_a(buf):
        dts = _softplus(dt_ref[0] + dtb_ref[...])
        a_row = -jnp.exp(alog_ref[...]) * math.log2(math.e)
        actb = xact_ref[0]
        for c in reversed(_SSD_CHUNKS):
            buf.put(c, actb, dts, a_row, tri_ref[1], 1)

    def stage_b(buf):
        half = SSM_HEADS * SSM_HEAD_DIM // SSM_GROUPS
        for c in reversed(_SSD_CHUNKS):
            r = slice(c * SSM_CHUNK, (c + 1) * SSM_CHUNK)
            yb = buf.take(c, st_ref)
            xs = xs_ref[0, r, :].astype(F32)
            y = (yf_ref[0, r, :] + yb + dsk_ref[...] * xs) * _silu(z_ref[0, r, :].astype(F32))
            for g in range(SSM_GROUPS):
                cols = slice(g * half, (g + 1) * half)
                y_ref[0, r, cols] = _rms(y[:, cols], ng_ref[:, cols]).astype(BF16)

    bufs = (_SsdStage(*stage_refs[:_N_STAGE_REFS]), _SsdStage(*stage_refs[_N_STAGE_REFS:]))
    _staged(j, nt, stage_a, stage_b, bufs)


def _ssd(xbc, dt, z, cw, cb, alog, dtb, dsk, ng, tri3, nct):
    b, t, c = xbc.shape
    nt = t // ROW_TILE
    hb = ROW_TILE // HALO
    nh = t // HALO
    d_in = SSM_HEADS * SSM_HEAD_DIM
    scratch = [pltpu.VMEM((SSM_STATE, d_in), F32)] + 2 * _SsdStage.shapes(d_in)

    def ta(j):
        return jnp.minimum(j, nt - 1)

    def tb(j):
        return jnp.maximum(j - 1, 0)

    def tile(width, pos):
        return pl.BlockSpec((1, ROW_TILE, width), lambda bi, j: (bi, pos(j), 0))

    xact, yf = pl.pallas_call(
        functools.partial(_ssd_f_kernel, nct, nt),
        grid=(b, nt + 1),
        in_specs=[
            tile(c, ta),
            pl.BlockSpec((1, HALO, c), lambda bi, j: (bi, jnp.maximum(ta(j) * hb - 1, 0), 0)),
            pl.BlockSpec((1, HALO, c), lambda bi, j: (bi, jnp.minimum((ta(j) + 1) * hb, nh - 1), 0)),
            tile(LANES, ta),
            _const_spec(cw.shape), _const_spec(cb.shape), _const_spec(alog.shape),
            _const_spec(dtb.shape), _const_spec(tri3.shape),
        ],
        out_specs=[tile(c, ta), tile(d_in, tb)],
        out_shape=[jax.ShapeDtypeStruct((b, t, c), BF16), jax.ShapeDtypeStruct((b, t, d_in), F32)],
        scratch_shapes=scratch,
        compiler_params=_params("arbitrary", "arbitrary"),
        name="ssd_fwd",
    )(xbc, xbc, xbc, dt, cw, cb, alog, dtb, tri3)

    def rev(s):
        return jnp.where(s < nct, nct - 1 - s, nt - 1 - (s - nct))

    ra = lambda j: rev(ta(j))
    rb = lambda j: rev(tb(j))
    return pl.pallas_call(
        functools.partial(_ssd_b_kernel, nt),
        grid=(b, nt + 1),
        in_specs=[
            tile(c, ra), tile(LANES, ra), tile(d_in, rb), tile(d_in, rb), tile(d_in, rb),
            _const_spec(alog.shape), _const_spec(dtb.shape), _const_spec(dsk.shape),
            _const_spec(ng.shape), _const_spec(tri3.shape),
        ],
        out_specs=tile(d_in, rb),
        out_shape=jax.ShapeDtypeStruct((b, t, d_in), BF16),
        scratch_shapes=scratch,
        compiler_params=_params("arbitrary", "arbitrary"),
        name="ssd_bwd",
    )(xact, dt, xact, z, yf, alog, dtb, dsk, ng, tri3)


def _attn_kernel(nct, nsub, q_ref, k_ref, v_ref, o_ref, vp_ref):
    j = pl.program_id(2)
    nkeys = k_ref.shape[1]
    nsteps = nkeys // ROW_TILE // nsub
    first = lax.broadcasted_iota(jnp.int32, (ROW_TILE, LANES), 1) < MLA_V
    one_lane = (MLA_V, 0)

    @pl.when(j == 0)
    def _():
        v = v_ref[0]
        lane = lax.broadcasted_iota(jnp.int32, v.shape, 1)
        for hh in range(2):
            own = (lane < MLA_V) if hh == 0 else (lane >= MLA_V)
            fill = jnp.where(lane == one_lane[hh], 1.0, 0.0).astype(BF16)
            vp_ref[hh] = jnp.where(own, v, fill)

    def attend(sub, nk):
        rs = slice(sub * ROW_TILE, (sub + 1) * ROW_TILE)
        outs = []
        for hh in range(2):
            sl = slice(hh * LANES, (hh + 1) * LANES)
            s = _dot_nt(q_ref[0, rs, sl], k_ref[0, 0:nk, sl])
            p = jnp.exp2(s - jnp.max(s, axis=-1, keepdims=True))
            r = _dot(p.astype(BF16), vp_ref[hh, 0:nk, :])
            outs.append(r / r[:, one_lane[hh]:one_lane[hh] + 1])
        o_ref[0, rs, :] = jnp.where(first, outs[0], outs[1]).astype(BF16)

    def keys_of(tile):
        return nct * ROW_TILE if tile < nct else nkeys

    variants = {}
    for step in range(nsteps):
        variants.setdefault(tuple(keys_of(step * nsub + sub) for sub in range(nsub)), []).append(step)
    for keys, steps in variants.items():
        cond = functools.reduce(jnp.logical_or, [j == st for st in steps])

        def branch(keys=keys):
            for sub, nk in enumerate(keys):
                attend(sub, nk)

        pl.when(cond)(branch)


def _attention(q, k, v, nct):
    b, t, _ = q.shape
    nt = t // ROW_TILE
    pairs = MLA_HEADS // 2
    nsub = next(n for n in ATTN_TILES_PER_STEP if nt % n == 0)
    rows = nsub * ROW_TILE
    return pl.pallas_call(
        functools.partial(_attn_kernel, nct, nsub),
        grid=(b, pairs, nt // nsub),
        in_specs=[
            pl.BlockSpec((1, rows, 2 * LANES), lambda bi, p, j: (bi, j, p)),
            pl.BlockSpec((1, t, 2 * LANES), lambda bi, p, j: (bi, 0, p)),
            pl.BlockSpec((1, t, LANES), lambda bi, p, j: (bi, 0, p)),
        ],
        out_specs=pl.BlockSpec((1, rows, LANES), lambda bi, p, j: (bi, j, p)),
        out_shape=jax.ShapeDtypeStruct((b, t, pairs * LANES), BF16),
        scratch_shapes=[pltpu.VMEM((2, t, LANES), BF16)],
        compiler_params=_params("parallel", "arbitrary", "arbitrary"),
        name="mla_attn",
    )(q, k, v)


def _fmix_kernel(nct, uv_ref, cl_ref, sl_ref, cc_ref, sc_ref, o_ref):
    half = uv_ref.shape[2] // 2
    nctx = nct * ROW_TILE
    nlat = uv_ref.shape[1] - nctx
    for r in range(0, nctx, ROW_TILE):
        o_ref[0, r:r + ROW_TILE, :] = (
            _dot(cc_ref[r:r + ROW_TILE, :], uv_ref[0, 0:nctx, 0:half])
            + _dot(sc_ref[r:r + ROW_TILE, :], uv_ref[0, 0:nctx, half:])).astype(BF16)
    for r in range(0, nlat, ROW_TILE):
        o_ref[0, nctx + r:nctx + r + ROW_TILE, :] = (
            _dot(cl_ref[r:r + ROW_TILE, :], uv_ref[0, nctx:, 0:half])
            + _dot(sl_ref[r:r + ROW_TILE, :], uv_ref[0, nctx:, half:])).astype(BF16)


def _fmix(uv, cl, sl, cc, sc, nct):
    b, t, w = uv.shape
    return pl.pallas_call(
        functools.partial(_fmix_kernel, nct),
        grid=(b,),
        in_specs=[
            pl.BlockSpec((1, t, w), lambda bi: (bi, 0, 0)),
            _const_spec(cl.shape), _const_spec(sl.shape), _const_spec(cc.shape), _const_spec(sc.shape),
        ],
        out_specs=pl.BlockSpec((1, t, w // 2), lambda bi: (bi, 0, 0)),
        out_shape=jax.ShapeDtypeStruct((b, t, w // 2), BF16),
        compiler_params=_params("parallel"),
        name="fourier_mix",
    )(uv, cl, sl, cc, sc)


def _post_kernel(nct, nt, last, x_ref, mod_ref, ys_ref, sc_ref, top_ref, bot_ref, at_ref, fo_ref,
                 gt_ref, scw_ref, wb_ref, wo_ref, g2_ref, w1_ref, w2_ref, fg_ref, o_ref):
    j = pl.program_id(1)

    def body():
        mod = mod_ref[0]
        w = sc_ref.shape[2] // 3
        top_ok = jnp.logical_and(j != 0, j != nct)
        bot_ok = jnp.logical_and(j != nct - 1, j != nt - 1)

        def cx(ref):
            return ref[0, :, w:2 * w].astype(F32) * ref[0, :, 2 * w:3 * w].astype(F32)

        conv = _halo_conv(cx(top_ref), cx(sc_ref), cx(bot_ref), top_ok, bot_ok, scw_ref[...], SC_CONV)
        branches = [ys_ref[0], (sc_ref[0, :, 0:w].astype(F32) * conv).astype(BF16), at_ref[0],
                    fo_ref[0]]
        d = x_ref.shape[2]
        merged = None
        for k in range(N_BRANCHES):
            term = gt_ref[0, :, k * d:(k + 1) * d].astype(F32) * _dot(branches[k], wb_ref[k])
            merged = term if merged is None else merged + term
        x1 = x_ref[0] + mod[2:3] * _dot(merged.astype(BF16), wo_ref[...])
        h2 = (_rms(x1, g2_ref[...]) * (1.0 + mod[4:5]) + mod[3:4]).astype(BF16)
        hid = jnp.maximum(_dot(h2, w1_ref[...]), 0.0)
        x2 = x1 + mod[5:6] * _dot((hid * hid).astype(BF16), w2_ref[...])
        o_ref[0] = _rms(x2, fg_ref[...]) if last else x2

    if last:
        pl.when(j >= nct)(body)
    else:
        body()


def _post(x, mod, ys, sc, at, fo, gt, scw, wb, wo, g2, w1, w2, fg, nct, last):
    b, t, d = x.shape
    nt = t // ROW_TILE
    nb = mod.shape[0] - 1
    hb = ROW_TILE // HALO
    nh = t // HALO
    tile = lambda width: pl.BlockSpec((1, ROW_TILE, width), lambda bi, j: (bi, j, 0))
    scw_w = sc.shape[2]
    if last:
        out_spec = pl.BlockSpec((1, ROW_TILE, d), lambda bi, j: (bi, jnp.maximum(j - nct, 0), 0))
        out_shape = jax.ShapeDtypeStruct((b, t - nct * ROW_TILE, d), F32)
        alias = {}
    else:
        out_spec, out_shape, alias = tile(d), jax.ShapeDtypeStruct((b, t, d), F32), {0: 0}
    return pl.pallas_call(
        functools.partial(_post_kernel, nct, nt, last),
        grid=(b, nt),
        in_specs=[
            tile(d),
            pl.BlockSpec((1, 6, d), lambda bi, j: (jnp.where(j < nct, nb, bi), 0, 0)),
            tile(ys.shape[2]), tile(scw_w),
            pl.BlockSpec((1, HALO, scw_w), lambda bi, j: (bi, jnp.maximum(j * hb - 1, 0), 0)),
            pl.BlockSpec((1, HALO, scw_w), lambda bi, j: (bi, jnp.minimum((j + 1) * hb, nh - 1), 0)),
            tile(at.shape[2]), tile(fo.shape[2]), tile(gt.shape[2]),
            _const_spec(scw.shape), _const_spec(wb.shape), _const_spec(wo.shape),
            _const_spec(g2.shape), _const_spec(w1.shape), _const_spec(w2.shape),
            _const_spec(fg.shape),
        ],
        out_specs=out_spec,
        out_shape=out_shape,
        input_output_aliases=alias,
        compiler_params=_params("parallel", "arbitrary"),
        name="post_merge_mlp",
    )(x, mod, ys, sc, sc, sc, at, fo, gt, scw, wb, wo, g2, w1, w2, fg)


def _rope_table(seq, ctx):
    nf = MLA_ROPE // 4
    pos = jnp.arange(seq)
    inv = ROPE_THETA ** (-jnp.arange(nf, dtype=F32) / nf)
    ang = jnp.stack([(pos // GRID_W).astype(F32)[:, None] * inv,
                     (pos % GRID_W).astype(F32)[:, None] * inv], axis=1)
    cos, sin = jnp.cos(ang), jnp.sin(ang)
    zero = jnp.zeros_like(sin)
    cos_l = jnp.stack([cos, cos], axis=2).reshape(seq, MLA_ROPE)
    sa_l = jnp.stack([-sin, zero], axis=2).reshape(seq, MLA_ROPE)
    sb_l = jnp.stack([zero, sin], axis=2).reshape(seq, MLA_ROPE)

    def lanes(v, fill):
        full = jnp.full((ctx + seq, LANES), fill, F32)
        return full.at[ctx:, _KR_LANE:_KR_LANE + MLA_ROPE].set(v)

    return jnp.concatenate([lanes(cos_l, 1.0), lanes(sa_l, 0.0), lanes(sb_l, 0.0)], axis=1)


def _dft_mats(n):
    w = math.gcd(n, 64)
    k = jnp.arange(n, dtype=jnp.int32)[:, None]

    def table(cols):
        ang = ((k * cols[None, :]) % n).astype(F32) * (2.0 * math.pi / n)
        return jnp.cos(ang), jnp.sin(ang)

    ca, sa = table(jnp.arange(n // w, dtype=jnp.int32) * w)
    cb, sb = table(jnp.arange(w, dtype=jnp.int32))
    s = n ** -0.5
    cos = ca[:, :, None] * cb[:, None, :] - sa[:, :, None] * sb[:, None, :]
    sin = sa[:, :, None] * cb[:, None, :] + ca[:, :, None] * sb[:, None, :]
    return (cos * s).reshape(n, n).astype(BF16), (-sin * s).reshape(n, n).astype(BF16)


def _channel_dft(width):
    gd = width // FNET_GROUPS
    idx = np.arange(gd)
    ang = 2.0 * np.pi * ((idx[:, None] * idx[None, :]) % gd) / gd
    eye = np.eye(FNET_GROUPS)
    c = np.kron(eye, np.cos(ang)) / math.sqrt(gd)
    s = np.kron(eye, np.sin(ang)) / math.sqrt(gd)
    return jnp.asarray(np.concatenate([c, s], axis=1), BF16)


def _head_pad(w, width, used):
    k = w.shape[0]
    w = w.reshape(k, MLA_HEADS, width)[:, :, :used]
    return jnp.pad(w, ((0, 0), (0, 0), (0, LANES - used))).reshape(k, MLA_HEADS * LANES)


def kernel(x, c, ctx, c_ctx, ada_w, ada_b, norm1_g, norm2_g, w_in, ssm_conv_w, ssm_conv_b, ssm_a_log, ssm_dt_bias, ssm_d, ssm_norm_g, sc_conv_w, mla_qa_g, mla_wqb, mla_kva_g, mla_wkvb, w_branch, w_out, mlp_w1, mlp_w2, final_norm_g):
    b, seq, d = x.shape
    nctx = ctx.shape[1]
    depth = ada_w.shape[0]
    assert seq % ROW_TILE == 0 and nctx % ROW_TILE == 0 and seq % GRID_W == 0
    nct = nctx // ROW_TILE
    d_in = SSM_HEADS * SSM_HEAD_DIM

    xcat = jnp.concatenate([ctx, x], axis=1)
    mods = _modulation(jnp.concatenate([c, c_ctx[None]], axis=0), ada_w, ada_b)

    tab = _rope_table(seq, nctx)
    cl, sl = _dft_mats(seq)
    cc, sc_m = _dft_mats(nctx)
    wc = _channel_dft(d // 2)
    tri_np = np.tril(np.ones((SSM_CHUNK, SSM_CHUNK), np.float32))
    tri3 = jnp.asarray(np.stack([np.tile(tri_np, (1, 3)), np.tile(tri_np.T, (1, 3))]), BF16)

    def lane_pad(v):
        return jnp.pad(v.reshape(1, -1), ((0, 0), (0, LANES - v.size)))

    for i in range(depth):
        wz, wxbc, wdt, wsc, wqa, wkva, wkr, wfn, wgt = jnp.split(
            w_in[i], np.cumsum([512, 1024, 16, 1536, 384, 256, 32, 512])[:].tolist(), axis=1)
        small = jnp.zeros((d, LANES), F32)
        small = small.at[:, _DT_LANE:_DT_LANE + 16].set(wdt).at[:, _KR_LANE:_KR_LANE + MLA_ROPE].set(wkr)
        w1 = jnp.concatenate([wz, wxbc, wsc, wqa, wkva, wfn, wgt * -math.log2(math.e), small],
                             axis=1).astype(BF16)
        q_scale = (MLA_NOPE + MLA_ROPE) ** -0.5 * math.log2(math.e)
        wqb = _head_pad(mla_wqb[i] * q_scale, MLA_NOPE + MLA_ROPE, MLA_NOPE + MLA_ROPE).astype(BF16)
        wkk = _head_pad(mla_wkvb[i], MLA_NOPE + MLA_V, MLA_NOPE).astype(BF16)
        wkv = mla_wkvb[i].reshape(-1, MLA_HEADS, MLA_NOPE + MLA_V)[:, :, MLA_NOPE:]
        wkv = wkv.reshape(-1, MLA_HEADS * MLA_V).astype(BF16)

        z, xbc, sc, gt, dt, q, k, v, uv = _pre(
            xcat, mods[i], norm1_g[i][None], w1, mla_qa_g[i][None], wqb, mla_kva_g[i][None], wkk, wkv,
            wc, tab, nct)
        ys = _ssd(xbc, dt, z, ssm_conv_w[i], ssm_conv_b[i][None], lane_pad(ssm_a_log[i]),
                  lane_pad(ssm_dt_bias[i]), jnp.repeat(ssm_d[i], SSM_HEAD_DIM)[None],
                  ssm_norm_g[i][None], tri3, nct)
        at = _attention(q, k, v, nct)
        fo = _fmix(uv, cl, sl, cc, sc_m, nct)
        xcat = _post(xcat, mods[i], ys, sc, at, fo, gt, sc_conv_w[i], w_branch[i].astype(BF16),
                     w_out[i].astype(BF16), norm2_g[i][None], mlp_w1[i].astype(BF16),
                     mlp_w2[i].astype(BF16), final_norm_g[None], nct, i == depth - 1)
    return xcat
```

```python
import functools
import math

import jax
import jax.numpy as jnp
import numpy as np
from jax import lax
from jax.experimental import pallas as pl
from jax.experimental.pallas import tpu as pltpu

F32 = jnp.float32
BF16 = jnp.bfloat16

LANES = 128
BF16_SUBLANES = 16
VMEM_LIMIT_BYTES = 56 * 1024 * 1024

EPS = 1e-6
ROPE_THETA = 10000.0
GRID_W = 64
N_BRANCHES = 4
SSM_HEADS = 8
SSM_HEAD_DIM = 64
SSM_GROUPS = 2
SSM_HPG = SSM_HEADS // SSM_GROUPS
SSM_STATE = 128
SSM_CHUNK = 128
SSM_CONV = 5
SC_CONV = 3
MLA_HEADS = 8
MLA_NOPE = 64
MLA_ROPE = 32
MLA_V = 64
FNET_GROUPS = 4

ROW_TILE = 256
HALO = BF16_SUBLANES
ATTN_TILES_PER_STEP = (9, 3, 2, 1)


def _const_spec(shape):
    nd = len(shape)
    return pl.BlockSpec(shape, lambda *_: (0,) * nd, pipeline_mode=pl.Buffered(1))


def _params(*sem, flags=None):
    return pltpu.CompilerParams(dimension_semantics=sem, vmem_limit_bytes=VMEM_LIMIT_BYTES, flags=flags)


def _sigmoid(v):
    return 1.0 / (1.0 + jnp.exp(-v))


def _silu(v):
    return v * _sigmoid(v)


def _softplus(v):
    return jnp.maximum(v, 0.0) + jnp.log1p(jnp.exp(-jnp.abs(v)))


def _rms(v, g):
    return v * lax.rsqrt(jnp.mean(v * v, axis=-1, keepdims=True) + EPS) * g


def _dot(a, b):
    return jnp.dot(a, b, preferred_element_type=F32)


def _dot_exact(a, b):
    return jnp.dot(a, b, preferred_element_type=F32, precision=lax.Precision.HIGHEST)


def _dot_nt(a, b):
    return lax.dot_general(a, b, (((1,), (1,)), ((), ())), preferred_element_type=F32)


def _mod_kernel(c_ref, w_ref, b_ref, o_ref):
    o_ref[0, 0] = _dot_exact(_silu(c_ref[...]), w_ref[0]) + b_ref[0, 0]


def _modulation(cc, ada_w, ada_b):
    depth, d, d6 = ada_w.shape
    n = cc.shape[0]
    out = pl.pallas_call(
        _mod_kernel,
        grid=(depth, d6 // d),
        in_specs=[
            pl.BlockSpec((n, d), lambda i, k: (0, 0)),
            pl.BlockSpec((1, d, d), lambda i, k: (i, 0, k)),
            pl.BlockSpec((1, 1, 1, d), lambda i, k: (i, k, 0, 0)),
        ],
        out_specs=pl.BlockSpec((1, 1, n, d), lambda i, k: (i, k, 0, 0)),
        out_shape=jax.ShapeDtypeStruct((depth, d6 // d, n, d), F32),
        compiler_params=_params("arbitrary", "arbitrary"),
        name="adaln_mod",
    )(cc, ada_w, ada_b.reshape(depth, d6 // d, 1, d))
    return jnp.transpose(out, (0, 2, 1, 3))


_C_Z, _C_XBC, _C_SC, _C_QA, _C_KVA, _C_FN, _C_GT, _C_SM, _C_END = (
    0, 512, 1536, 3072, 3456, 3712, 4224, 8320, 8448)
_DT_LANE = 0
_KR_LANE = MLA_NOPE


def _rope(v, tab):
    tc = tab[:, 0:LANES]
    tsa = tab[:, LANES:2 * LANES]
    tsb = tab[:, 2 * LANES:3 * LANES]
    half = MLA_ROPE // 4
    return v * tc + pltpu.roll(v, LANES - half, 1) * tsa + pltpu.roll(v, half, 1) * tsb


def _pre_kernel(nt, x_ref, mod_ref, g1_ref, w_ref, qag_ref, wqb_ref, kvag_ref, wkk_ref, wkv_ref,
                wc_ref, tab_ref,
                z_ref, xbc_ref, sc_ref, gt_ref, dt_ref, q_ref, k_ref, v_ref, uv_ref,
                qn0_ref, kvn0_ref, fn0_ref, sm0_ref, qn1_ref, kvn1_ref, fn1_ref, sm1_ref):
    j = pl.program_id(1)

    def stage_a(qn_ref, kvn_ref, fn_ref, sm_ref):
        mod = mod_ref[0]
        hb = (_rms(x_ref[0], g1_ref[...]) * (1.0 + mod[1:2]) + mod[0:1]).astype(BF16)

        def proj(lo, hi):
            return _dot(hb, w_ref[:, lo:hi])

        qn_ref[...] = _rms(proj(_C_QA, _C_KVA), qag_ref[...]).astype(BF16)
        kvn_ref[...] = _rms(proj(_C_KVA, _C_FN), kvag_ref[...]).astype(BF16)
        fn_ref[...] = proj(_C_FN, _C_GT).astype(BF16)
        small = proj(_C_SM, _C_END)
        sm_ref[...] = small
        dt_ref[0] = small
        z_ref[0] = proj(_C_Z, _C_XBC).astype(BF16)
        xbc_ref[0] = proj(_C_XBC, _C_SC).astype(BF16)
        sc_ref[0] = proj(_C_SC, _C_QA).astype(BF16)
        gt_ref[0] = (1.0 / (1.0 + jnp.exp2(proj(_C_GT, _C_SM)))).astype(BF16)

    def stage_b(qn_ref, kvn_ref, fn_ref, sm_ref):
        tab = tab_ref[...]
        q = _dot(qn_ref[...], wqb_ref[...])
        for hd in range(MLA_HEADS):
            sl = slice(hd * LANES, (hd + 1) * LANES)
            q_ref[0, :, sl] = _rope(q[:, sl], tab).astype(BF16)
        kvn = kvn_ref[...]
        kn = _dot(kvn, wkk_ref[...])
        small = sm_ref[...]
        lane = lax.broadcasted_iota(jnp.int32, small.shape, 1)
        kr = _rope(jnp.where(lane >= _KR_LANE, small, 0.0), tab)
        for hd in range(MLA_HEADS):
            sl = slice(hd * LANES, (hd + 1) * LANES)
            k_ref[0, :, sl] = (kn[:, sl] + kr).astype(BF16)
        v_ref[0] = _dot(kvn, wkv_ref[...]).astype(BF16)
        uv_ref[0] = _dot(fn_ref[...], wc_ref[...]).astype(BF16)

    bufs = ((qn0_ref, kvn0_ref, fn0_ref, sm0_ref), (qn1_ref, kvn1_ref, fn1_ref, sm1_ref))
    variants = {}
    for step in range(nt + 1):
        variants.setdefault((step < nt, step >= 1, step % 2), []).append(step)
    for (do_a, do_b, parity), steps in variants.items():
        cond = functools.reduce(jnp.logical_or, [j == st for st in steps])

        def branch(do_a=do_a, do_b=do_b, parity=parity):
            if do_b:
                stage_b(*bufs[1 - parity])
            if do_a:
                stage_a(*bufs[parity])

        pl.when(cond)(branch)


def _pre(x, mod, g1, w1, qag, wqb, kvag, wkk, wkv, wc, tab, nct):
    b, t, d = x.shape
    nt = t // ROW_TILE
    nb = mod.shape[0] - 1

    def ta(j):
        return jnp.minimum(j, nt - 1)

    def tb(j):
        return jnp.maximum(j - 1, 0)

    def row(width, tile, dtype=BF16):
        return (pl.BlockSpec((1, ROW_TILE, width), lambda bi, j: (bi, tile(j), 0)),
                jax.ShapeDtypeStruct((b, t, width), dtype))

    outs = [row(512, ta), row(1024, ta), row(1536, ta), row(4096, ta), row(LANES, ta, F32),
            row(1024, tb), row(1024, tb), row(512, tb), row(1024, tb)]
    stage_bufs = [pltpu.VMEM((ROW_TILE, _C_KVA - _C_QA), BF16), pltpu.VMEM((ROW_TILE, _C_FN - _C_KVA), BF16),
                  pltpu.VMEM((ROW_TILE, _C_GT - _C_FN), BF16), pltpu.VMEM((ROW_TILE, LANES), F32)]
    return pl.pallas_call(
        functools.partial(_pre_kernel, nt),
        grid=(b, nt + 1),
        in_specs=[
            pl.BlockSpec((1, ROW_TILE, d), lambda bi, j: (bi, ta(j), 0)),
            pl.BlockSpec((1, 6, d), lambda bi, j: (jnp.where(ta(j) < nct, nb, bi), 0, 0)),
            _const_spec(g1.shape), _const_spec(w1.shape), _const_spec(qag.shape),
            _const_spec(wqb.shape), _const_spec(kvag.shape), _const_spec(wkk.shape),
            _const_spec(wkv.shape), _const_spec(wc.shape),
            pl.BlockSpec((ROW_TILE, 3 * LANES), lambda bi, j: (tb(j), 0)),
        ],
        out_specs=[o[0] for o in outs],
        out_shape=[o[1] for o in outs],
        scratch_shapes=stage_bufs + stage_bufs,
        compiler_params=_params("arbitrary", "arbitrary"),
        name="pre_proj",
    )(x, mod, g1, w1, qag, wqb, kvag, wkk, wkv, wc, tab)


def _halo_conv(top, mid, bot, top_ok, bot_ok, w, ksize):
    rows = mid.shape[0]
    full = jnp.concatenate([jnp.where(top_ok, top, 0.0), mid, jnp.where(bot_ok, bot, 0.0)], axis=0)
    n = full.shape[0]
    pad = (ksize - 1) // 2
    acc = None
    for k in range(ksize):
        d = k - pad
        sh = full if d == 0 else pltpu.roll(full, (-d) % n, 0)
        term = sh[HALO:HALO + rows] * w[k:k + 1]
        acc = term if acc is None else acc + term
    return acc


def _split3(v, axis):
    hi = v.astype(BF16)
    r1 = v - hi.astype(F32)
    mid = r1.astype(BF16)
    lo = (r1 - mid.astype(F32)).astype(BF16)
    return jnp.concatenate([hi, mid, lo], axis=axis)


def _ssd_chunk(xs, bm, cm, dts, a_row, tri3, st_ref, direction):
    q = xs.shape[0]
    off = SSM_HEADS * direction
    la = dts * a_row
    cum = _dot(tri3, _split3(la, 0))
    cum_t = cum.T[off:off + SSM_HEADS]
    lane = lax.broadcasted_iota(jnp.int32, (q, LANES), 1)
    first_half = lane < SSM_HEAD_DIM

    def spread(v):
        return [jnp.broadcast_to(v[:, off + hd:off + hd + 1], (q, LANES)) for hd in range(SSM_HEADS)]

    def pairs(blocks):
        return jnp.concatenate([jnp.where(first_half, blocks[2 * p], blocks[2 * p + 1])
                                for p in range(SSM_HEADS // 2)], axis=1)

    cum_blk = spread(cum)
    cum_w = jnp.concatenate(cum_blk, axis=1)
    cum_x = pairs(cum_blk)
    dt_x = pairs(spread(dts))
    last = q - 1 if direction == 0 else 0
    tot_x = cum_x[last:last + 1]
    xdt = xs * dt_x
    xw = (xdt * jnp.exp2(tot_x - cum_x)).astype(BF16)
    xdt_b = xdt.astype(BF16)
    keep = tri3[:, 0:q] > 0
    y_pairs = []
    y_off = []
    for g in range(SSM_GROUPS):
        cg = cm[:, g * SSM_STATE:(g + 1) * SSM_STATE]
        bg = bm[:, g * SSM_STATE:(g + 1) * SSM_STATE]
        cb = _dot_nt(cg, bg)
        yd = []
        for hl in range(SSM_HPG):
            hd = g * SSM_HPG + hl
            ci = cum_w[:, hd * LANES:(hd + 1) * LANES]
            cj = cum_t[hd:hd + 1, :]
            seg = jnp.exp2(jnp.where(keep, ci - cj, -jnp.inf))
            m = (cb * seg).astype(BF16)
            pair = hd // 2
            yd.append(_dot(m, xdt_b[:, pair * LANES:(pair + 1) * LANES]))
        for pr in range(SSM_HPG // 2):
            y_pairs.append(jnp.where(first_half, yd[2 * pr], yd[2 * pr + 1]))
        cols = slice(g * SSM_HPG * SSM_HEAD_DIM, (g + 1) * SSM_HPG * SSM_HEAD_DIM)
        st = st_ref[:, cols]
        y_off.append(_dot(cg, st.astype(BF16)))
        upd = _dot(bg.astype(F32).T.astype(BF16), xw[:, cols])
        st_ref[:, cols] = st * jnp.exp2(tot_x[:, cols]) + upd
    y_diag = jnp.concatenate(y_pairs, axis=1)
    return y_diag + jnp.concatenate(y_off, axis=1) * jnp.exp2(cum_x)


def _ssd_f_kernel(nct, nt, xbc_ref, top_ref, bot_ref, dt_ref, cw_ref, cb_ref, alog_ref, dtb_ref,
                  tri_ref, xact_ref, yf_ref, st_ref):
    j = pl.program_id(1)

    @pl.when(j == 0)
    def _():
        st_ref[...] = jnp.zeros_like(st_ref)

    top_ok = jnp.logical_and(j != 0, j != nct)
    bot_ok = jnp.logical_and(j != nct - 1, j != nt - 1)
    conv = _halo_conv(top_ref[0].astype(F32), xbc_ref[0].astype(F32), bot_ref[0].astype(F32),
                      top_ok, bot_ok, cw_ref[...], SSM_CONV)
    act = _silu(conv + cb_ref[...])
    actb = act.astype(BF16)
    xact_ref[0] = actb
    dts = _softplus(dt_ref[0] + dtb_ref[...])
    a_row = -jnp.exp(alog_ref[...]) * math.log2(math.e)
    d_in = SSM_HEADS * SSM_HEAD_DIM
    for c in range(ROW_TILE // SSM_CHUNK):
        r = slice(c * SSM_CHUNK, (c + 1) * SSM_CHUNK)
        yf_ref[0, r, :] = _ssd_chunk(act[r, 0:d_in], actb[r, d_in:d_in + 256], actb[r, d_in + 256:],
                                     dts[r], a_row, tri_ref[0], st_ref, 0)


def _ssd_b_kernel(xact_ref, dt_ref, z_ref, yf_ref, alog_ref, dtb_ref, dsk_ref, ng_ref,
                  tri_ref, y_ref, st_ref):
    j = pl.program_id(1)

    @pl.when(j == 0)
    def _():
        st_ref[...] = jnp.zeros_like(st_ref)

    actb = xact_ref[0]
    dts = _softplus(dt_ref[0] + dtb_ref[...])
    a_row = -jnp.exp(alog_ref[...]) * math.log2(math.e)
    d_in = SSM_HEADS * SSM_HEAD_DIM
    half = d_in // SSM_GROUPS
    for c in reversed(range(ROW_TILE // SSM_CHUNK)):
        r = slice(c * SSM_CHUNK, (c + 1) * SSM_CHUNK)
        xs = actb[r, 0:d_in].astype(F32)
        yb = _ssd_chunk(xs, actb[r, d_in:d_in + 256], actb[r, d_in + 256:], dts[r], a_row,
                        tri_ref[1], st_ref, 1)
        y = (yf_ref[0, r, :] + yb + dsk_ref[...] * xs) * _silu(z_ref[0, r, :].astype(F32))
        for g in range(SSM_GROUPS):
            cols = slice(g * half, (g + 1) * half)
            y_ref[0, r, cols] = _rms(y[:, cols], ng_ref[:, cols]).astype(BF16)


def _ssd(xbc, dt, z, cw, cb, alog, dtb, dsk, ng, tri3, nct):
    b, t, c = xbc.shape
    nt = t // ROW_TILE
    hb = ROW_TILE // HALO
    nh = t // HALO
    d_in = SSM_HEADS * SSM_HEAD_DIM
    tile = lambda width: pl.BlockSpec((1, ROW_TILE, width), lambda bi, j: (bi, j, 0))
    xact, yf = pl.pallas_call(
        functools.partial(_ssd_f_kernel, nct, nt),
        grid=(b, nt),
        in_specs=[
            tile(c),
            pl.BlockSpec((1, HALO, c), lambda bi, j: (bi, jnp.maximum(j * hb - 1, 0), 0)),
            pl.BlockSpec((1, HALO, c), lambda bi, j: (bi, jnp.minimum((j + 1) * hb, nh - 1), 0)),
            tile(LANES),
            _const_spec(cw.shape), _const_spec(cb.shape), _const_spec(alog.shape),
            _const_spec(dtb.shape), _const_spec(tri3.shape),
        ],
        out_specs=[tile(c), tile(d_in)],
        out_shape=[jax.ShapeDtypeStruct((b, t, c), BF16), jax.ShapeDtypeStruct((b, t, d_in), F32)],
        scratch_shapes=[pltpu.VMEM((SSM_STATE, d_in), F32)],
        compiler_params=_params("parallel", "arbitrary"),
        name="ssd_fwd",
    )(xbc, xbc, xbc, dt, cw, cb, alog, dtb, tri3)

    def rev(j):
        return jnp.where(j < nct, nct - 1 - j, nt - 1 - (j - nct))

    rtile = lambda width: pl.BlockSpec((1, ROW_TILE, width), lambda bi, j: (bi, rev(j), 0))
    return pl.pallas_call(
        _ssd_b_kernel,
        grid=(b, nt),
        in_specs=[
            rtile(c), rtile(LANES), rtile(d_in), rtile(d_in),
            _const_spec(alog.shape), _const_spec(dtb.shape), _const_spec(dsk.shape),
            _const_spec(ng.shape), _const_spec(tri3.shape),
        ],
        out_specs=rtile(d_in),
        out_shape=jax.ShapeDtypeStruct((b, t, d_in), BF16),
        scratch_shapes=[pltpu.VMEM((SSM_STATE, d_in), F32)],
        compiler_params=_params("parallel", "arbitrary"),
        name="ssd_bwd",
    )(xact, dt, z, yf, alog, dtb, dsk, ng, tri3)


def _attn_kernel(nct, nsub, q_ref, k_ref, v_ref, o_ref, vp_ref):
    j = pl.program_id(2)
    nkeys = k_ref.shape[1]
    nsteps = nkeys // ROW_TILE // nsub
    first = lax.broadcasted_iota(jnp.int32, (ROW_TILE, LANES), 1) < MLA_V
    one_lane = (MLA_V, 0)

    @pl.when(j == 0)
    def _():
        v = v_ref[0]
        lane = lax.broadcasted_iota(jnp.int32, v.shape, 1)
        for hh in range(2):
            own = (lane < MLA_V) if hh == 0 else (lane >= MLA_V)
            fill = jnp.where(lane == one_lane[hh], 1.0, 0.0).astype(BF16)
            vp_ref[hh] = jnp.where(own, v, fill)

    def attend(sub, nk):
        rs = slice(sub * ROW_TILE, (sub + 1) * ROW_TILE)
        outs = []
        for hh in range(2):
            sl = slice(hh * LANES, (hh + 1) * LANES)
            s = _dot_nt(q_ref[0, rs, sl], k_ref[0, 0:nk, sl])
            p = jnp.exp2(s - jnp.max(s, axis=-1, keepdims=True))
            r = _dot(p.astype(BF16), vp_ref[hh, 0:nk, :])
            outs.append(r / r[:, one_lane[hh]:one_lane[hh] + 1])
        o_ref[0, rs, :] = jnp.where(first, outs[0], outs[1]).astype(BF16)

    def keys_of(tile):
        return nct * ROW_TILE if tile < nct else nkeys

    variants = {}
    for step in range(nsteps):
        variants.setdefault(tuple(keys_of(step * nsub + sub) for sub in range(nsub)), []).append(step)
    for keys, steps in variants.items():
        cond = functools.reduce(jnp.logical_or, [j == st for st in steps])

        def branch(keys=keys):
            for sub, nk in enumerate(keys):
                attend(sub, nk)

        pl.when(cond)(branch)


def _attention(q, k, v, nct):
    b, t, _ = q.shape
    nt = t // ROW_TILE
    pairs = MLA_HEADS // 2
    nsub = next(n for n in ATTN_TILES_PER_STEP if nt % n == 0)
    rows = nsub * ROW_TILE
    return pl.pallas_call(
        functools.partial(_attn_kernel, nct, nsub),
        grid=(b, pairs, nt // nsub),
        in_specs=[
            pl.BlockSpec((1, rows, 2 * LANES), lambda bi, p, j: (bi, j, p)),
            pl.BlockSpec((1, t, 2 * LANES), lambda bi, p, j: (bi, 0, p)),
            pl.BlockSpec((1, t, LANES), lambda bi, p, j: (bi, 0, p)),
        ],
        out_specs=pl.BlockSpec((1, rows, LANES), lambda bi, p, j: (bi, j, p)),
        out_shape=jax.ShapeDtypeStruct((b, t, pairs * LANES), BF16),
        scratch_shapes=[pltpu.VMEM((2, t, LANES), BF16)],
        compiler_params=_params("parallel", "arbitrary", "arbitrary"),
        name="mla_attn",
    )(q, k, v)


def _fmix_kernel(nct, uv_ref, cl_ref, sl_ref, cc_ref, sc_ref, o_ref):
    half = uv_ref.shape[2] // 2
    nctx = nct * ROW_TILE
    nlat = uv_ref.shape[1] - nctx
    for r in range(0, nctx, ROW_TILE):
        o_ref[0, r:r + ROW_TILE, :] = (
            _dot(cc_ref[r:r + ROW_TILE, :], uv_ref[0, 0:nctx, 0:half])
            + _dot(sc_ref[r:r + ROW_TILE, :], uv_ref[0, 0:nctx, half:])).astype(BF16)
    for r in range(0, nlat, ROW_TILE):
        o_ref[0, nctx + r:nctx + r + ROW_TILE, :] = (
            _dot(cl_ref[r:r + ROW_TILE, :], uv_ref[0, nctx:, 0:half])
            + _dot(sl_ref[r:r + ROW_TILE, :], uv_ref[0, nctx:, half:])).astype(BF16)


def _fmix(uv, cl, sl, cc, sc, nct):
    b, t, w = uv.shape
    return pl.pallas_call(
        functools.partial(_fmix_kernel, nct),
        grid=(b,),
        in_specs=[
            pl.BlockSpec((1, t, w), lambda bi: (bi, 0, 0)),
            _const_spec(cl.shape), _const_spec(sl.shape), _const_spec(cc.shape), _const_spec(sc.shape),
        ],
        out_specs=pl.BlockSpec((1, t, w // 2), lambda bi: (bi, 0, 0)),
        out_shape=jax.ShapeDtypeStruct((b, t, w // 2), BF16),
        compiler_params=_params("parallel"),
        name="fourier_mix",
    )(uv, cl, sl, cc, sc)


def _post_kernel(nct, nt, last, x_ref, mod_ref, ys_ref, sc_ref, top_ref, bot_ref, at_ref, fo_ref,
                 gt_ref, scw_ref, wb_ref, wo_ref, g2_ref, w1_ref, w2_ref, fg_ref, o_ref):
    j = pl.program_id(1)

    def body():
        mod = mod_ref[0]
        w = sc_ref.shape[2] // 3
        top_ok = jnp.logical_and(j != 0, j != nct)
        bot_ok = jnp.logical_and(j != nct - 1, j != nt - 1)

        def cx(ref):
            return ref[0, :, w:2 * w].astype(F32) * ref[0, :, 2 * w:3 * w].astype(F32)

        conv = _halo_conv(cx(top_ref), cx(sc_ref), cx(bot_ref), top_ok, bot_ok, scw_ref[...], SC_CONV)
        branches = [ys_ref[0], (sc_ref[0, :, 0:w].astype(F32) * conv).astype(BF16), at_ref[0],
                    fo_ref[0]]
        d = x_ref.shape[2]
        merged = None
        for k in range(N_BRANCHES):
            term = gt_ref[0, :, k * d:(k + 1) * d].astype(F32) * _dot(branches[k], wb_ref[k])
            merged = term if merged is None else merged + term
        x1 = x_ref[0] + mod[2:3] * _dot(merged.astype(BF16), wo_ref[...])
        h2 = (_rms(x1, g2_ref[...]) * (1.0 + mod[4:5]) + mod[3:4]).astype(BF16)
        hid = jnp.maximum(_dot(h2, w1_ref[...]), 0.0)
        x2 = x1 + mod[5:6] * _dot((hid * hid).astype(BF16), w2_ref[...])
        o_ref[0] = _rms(x2, fg_ref[...]) if last else x2

    if last:
        pl.when(j >= nct)(body)
    else:
        body()


def _post(x, mod, ys, sc, at, fo, gt, scw, wb, wo, g2, w1, w2, fg, nct, last):
    b, t, d = x.shape
    nt = t // ROW_TILE
    nb = mod.shape[0] - 1
    hb = ROW_TILE // HALO
    nh = t // HALO
    tile = lambda width: pl.BlockSpec((1, ROW_TILE, width), lambda bi, j: (bi, j, 0))
    scw_w = sc.shape[2]
    if last:
        out_spec = pl.BlockSpec((1, ROW_TILE, d), lambda bi, j: (bi, jnp.maximum(j - nct, 0), 0))
        out_shape = jax.ShapeDtypeStruct((b, t - nct * ROW_TILE, d), F32)
        alias = {}
    else:
        out_spec, out_shape, alias = tile(d), jax.ShapeDtypeStruct((b, t, d), F32), {0: 0}
    return pl.pallas_call(
        functools.partial(_post_kernel, nct, nt, last),
        grid=(b, nt),
        in_specs=[
            tile(d),
            pl.BlockSpec((1, 6, d), lambda bi, j: (jnp.where(j < nct, nb, bi), 0, 0)),
            tile(ys.shape[2]), tile(scw_w),
            pl.BlockSpec((1, HALO, scw_w), lambda bi, j: (bi, jnp.maximum(j * hb - 1, 0), 0)),
            pl.BlockSpec((1, HALO, scw_w), lambda bi, j: (bi, jnp.minimum((j + 1) * hb, nh - 1), 0)),
            tile(at.shape[2]), tile(fo.shape[2]), tile(gt.shape[2]),
            _const_spec(scw.shape), _const_spec(wb.shape), _const_spec(wo.shape),
            _const_spec(g2.shape), _const_spec(w1.shape), _const_spec(w2.shape),
            _const_spec(fg.shape),
        ],
        out_specs=out_spec,
        out_shape=out_shape,
        input_output_aliases=alias,
        compiler_params=_params("parallel", "arbitrary"),
        name="post_merge_mlp",
    )(x, mod, ys, sc, sc, sc, at, fo, gt, scw, wb, wo, g2, w1, w2, fg)


def _rope_table(seq, ctx):
    nf = MLA_ROPE // 4
    pos = jnp.arange(seq)
    inv = ROPE_THETA ** (-jnp.arange(nf, dtype=F32) / nf)
    ang = jnp.stack([(pos // GRID_W).astype(F32)[:, None] * inv,
                     (pos % GRID_W).astype(F32)[:, None] * inv], axis=1)
    cos, sin = jnp.cos(ang), jnp.sin(ang)
    zero = jnp.zeros_like(sin)
    cos_l = jnp.stack([cos, cos], axis=2).reshape(seq, MLA_ROPE)
    sa_l = jnp.stack([-sin, zero], axis=2).reshape(seq, MLA_ROPE)
    sb_l = jnp.stack([zero, sin], axis=2).reshape(seq, MLA_ROPE)

    def lanes(v, fill):
        full = jnp.full((ctx + seq, LANES), fill, F32)
        return full.at[ctx:, _KR_LANE:_KR_LANE + MLA_ROPE].set(v)

    return jnp.concatenate([lanes(cos_l, 1.0), lanes(sa_l, 0.0), lanes(sb_l, 0.0)], axis=1)


def _dft_mats(n):
    w = math.gcd(n, 64)
    k = jnp.arange(n, dtype=jnp.int32)[:, None]

    def table(cols):
        ang = ((k * cols[None, :]) % n).astype(F32) * (2.0 * math.pi / n)
        return jnp.cos(ang), jnp.sin(ang)

    ca, sa = table(jnp.arange(n // w, dtype=jnp.int32) * w)
    cb, sb = table(jnp.arange(w, dtype=jnp.int32))
    s = n ** -0.5
    cos = ca[:, :, None] * cb[:, None, :] - sa[:, :, None] * sb[:, None, :]
    sin = sa[:, :, None] * cb[:, None, :] + ca[:, :, None] * sb[:, None, :]
    return (cos * s).reshape(n, n).astype(BF16), (-sin * s).reshape(n, n).astype(BF16)


def _channel_dft(width):
    gd = width // FNET_GROUPS
    idx = np.arange(gd)
    ang = 2.0 * np.pi * ((idx[:, None] * idx[None, :]) % gd) / gd
    eye = np.eye(FNET_GROUPS)
    c = np.kron(eye, np.cos(ang)) / math.sqrt(gd)
    s = np.kron(eye, np.sin(ang)) / math.sqrt(gd)
    return jnp.asarray(np.concatenate([c, s], axis=1), BF16)


def _head_pad(w, width, used):
    k = w.shape[0]
    w = w.reshape(k, MLA_HEADS, width)[:, :, :used]
    return jnp.pad(w, ((0, 0), (0, 0), (0, LANES - used))).reshape(k, MLA_HEADS * LANES)


def kernel(x, c, ctx, c_ctx, ada_w, ada_b, norm1_g, norm2_g, w_in, ssm_conv_w, ssm_conv_b, ssm_a_log, ssm_dt_bias, ssm_d, ssm_norm_g, sc_conv_w, mla_qa_g, mla_wqb, mla_kva_g, mla_wkvb, w_branch, w_out, mlp_w1, mlp_w2, final_norm_g):
    b, seq, d = x.shape
    nctx = ctx.shape[1]
    depth = ada_w.shape[0]
    assert seq % ROW_TILE == 0 and nctx % ROW_TILE == 0 and seq % GRID_W == 0
    nct = nctx // ROW_TILE
    d_in = SSM_HEADS * SSM_HEAD_DIM

    xcat = jnp.concatenate([ctx, x], axis=1)
    mods = _modulation(jnp.concatenate([c, c_ctx[None]], axis=0), ada_w, ada_b)

    tab = _rope_table(seq, nctx)
    cl, sl = _dft_mats(seq)
    cc, sc_m = _dft_mats(nctx)
    wc = _channel_dft(d // 2)
    tri_np = np.tril(np.ones((SSM_CHUNK, SSM_CHUNK), np.float32))
    tri3 = jnp.asarray(np.stack([np.tile(tri_np, (1, 3)), np.tile(tri_np.T, (1, 3))]), BF16)

    def lane_pad(v):
        return jnp.pad(v.reshape(1, -1), ((0, 0), (0, LANES - v.size)))

    for i in range(depth):
        wz, wxbc, wdt, wsc, wqa, wkva, wkr, wfn, wgt = jnp.split(
            w_in[i], np.cumsum([512, 1024, 16, 1536, 384, 256, 32, 512])[:].tolist(), axis=1)
        small = jnp.zeros((d, LANES), F32)
        small = small.at[:, _DT_LANE:_DT_LANE + 16].set(wdt).at[:, _KR_LANE:_KR_LANE + MLA_ROPE].set(wkr)
        w1 = jnp.concatenate([wz, wxbc, wsc, wqa, wkva, wfn, wgt * -math.log2(math.e), small],
                             axis=1).astype(BF16)
        q_scale = (MLA_NOPE + MLA_ROPE) ** -0.5 * math.log2(math.e)
        wqb = _head_pad(mla_wqb[i] * q_scale, MLA_NOPE + MLA_ROPE, MLA_NOPE + MLA_ROPE).astype(BF16)
        wkk = _head_pad(mla_wkvb[i], MLA_NOPE + MLA_V, MLA_NOPE).astype(BF16)
        wkv = mla_wkvb[i].reshape(-1, MLA_HEADS, MLA_NOPE + MLA_V)[:, :, MLA_NOPE:]
        wkv = wkv.reshape(-1, MLA_HEADS * MLA_V).astype(BF16)

        z, xbc, sc, gt, dt, q, k, v, uv = _pre(
            xcat, mods[i], norm1_g[i][None], w1, mla_qa_g[i][None], wqb, mla_kva_g[i][None], wkk, wkv,
            wc, tab, nct)
        ys = _ssd(xbc, dt, z, ssm_conv_w[i], ssm_conv_b[i][None], lane_pad(ssm_a_log[i]),
                  lane_pad(ssm_dt_bias[i]), jnp.repeat(ssm_d[i], SSM_HEAD_DIM)[None],
                  ssm_norm_g[i][None], tri3, nct)
        at = _attention(q, k, v, nct)
        fo = _fmix(uv, cl, sl, cc, sc_m, nct)
        xcat = _post(xcat, mods[i], ys, sc, at, fo, gt, sc_conv_w[i], w_branch[i].astype(BF16),
                     w_out[i].astype(BF16), norm2_g[i][None], mlp_w1[i].astype(BF16),
                     mlp_w2[i].astype(BF16), final_norm_g[None], nct, i == depth - 1)
    return xcat
```

```python
import functools
import math

import jax
import jax.numpy as jnp
import numpy as np
from jax import lax
from jax.experimental import pallas as pl
from jax.experimental.pallas import tpu as pltpu

F32 = jnp.float32
BF16 = jnp.bfloat16

LANES = 128
BF16_SUBLANES = 16
VMEM_LIMIT_BYTES = 56 * 1024 * 1024

EPS = 1e-6
ROPE_THETA = 10000.0
GRID_W = 64
N_BRANCHES = 4
SSM_HEADS = 8
SSM_HEAD_DIM = 64
SSM_GROUPS = 2
SSM_HPG = SSM_HEADS // SSM_GROUPS
SSM_STATE = 128
SSM_CHUNK = 128
SSM_CONV = 5
SC_CONV = 3
MLA_HEADS = 8
MLA_NOPE = 64
MLA_ROPE = 32
MLA_V = 64
FNET_GROUPS = 4

ROW_TILE = 256
HALO = BF16_SUBLANES
ATTN_SHIFT_MARGIN = 64.0
ATTN_TILES_PER_STEP = (9, 3, 2, 1)


def _const_spec(shape):
    nd = len(shape)
    return pl.BlockSpec(shape, lambda *_: (0,) * nd, pipeline_mode=pl.Buffered(1))


def _params(*sem, flags=None):
    return pltpu.CompilerParams(dimension_semantics=sem, vmem_limit_bytes=VMEM_LIMIT_BYTES, flags=flags)


def _sigmoid(v):
    return 1.0 / (1.0 + jnp.exp(-v))


def _silu(v):
    return v * _sigmoid(v)


def _softplus(v):
    return jnp.maximum(v, 0.0) + jnp.log1p(jnp.exp(-jnp.abs(v)))


def _rms(v, g):
    return v * lax.rsqrt(jnp.mean(v * v, axis=-1, keepdims=True) + EPS) * g


def _dot(a, b):
    return jnp.dot(a, b, preferred_element_type=F32)


def _dot_exact(a, b):
    return jnp.dot(a, b, preferred_element_type=F32, precision=lax.Precision.HIGHEST)


def _dot_nt(a, b):
    return lax.dot_general(a, b, (((1,), (1,)), ((), ())), preferred_element_type=F32)


def _mod_kernel(c_ref, w_ref, b_ref, o_ref):
    o_ref[0, 0] = _dot_exact(_silu(c_ref[...]), w_ref[0]) + b_ref[0, 0]


def _modulation(cc, ada_w, ada_b):
    depth, d, d6 = ada_w.shape
    n = cc.shape[0]
    out = pl.pallas_call(
        _mod_kernel,
        grid=(depth, d6 // d),
        in_specs=[
            pl.BlockSpec((n, d), lambda i, k: (0, 0)),
            pl.BlockSpec((1, d, d), lambda i, k: (i, 0, k)),
            pl.BlockSpec((1, 1, 1, d), lambda i, k: (i, k, 0, 0)),
        ],
        out_specs=pl.BlockSpec((1, 1, n, d), lambda i, k: (i, k, 0, 0)),
        out_shape=jax.ShapeDtypeStruct((depth, d6 // d, n, d), F32),
        compiler_params=_params("arbitrary", "arbitrary"),
        name="adaln_mod",
    )(cc, ada_w, ada_b.reshape(depth, d6 // d, 1, d))
    return jnp.transpose(out, (0, 2, 1, 3))


_C_Z, _C_XBC, _C_SC, _C_QA, _C_KVA, _C_FN, _C_GT, _C_SM, _C_END = (
    0, 512, 1536, 3072, 3456, 3712, 4224, 8320, 8448)
_DT_LANE = 0
_KR_LANE = MLA_NOPE


def _rope(v, tab):
    tc = tab[:, 0:LANES]
    tsa = tab[:, LANES:2 * LANES]
    tsb = tab[:, 2 * LANES:3 * LANES]
    half = MLA_ROPE // 4
    return v * tc + pltpu.roll(v, LANES - half, 1) * tsa + pltpu.roll(v, half, 1) * tsb


def _pre_kernel(nt, x_ref, mod_ref, g1_ref, w_ref, qag_ref, wqb_ref, kvag_ref, wkk_ref, wkv_ref,
                wc_ref, tab_ref,
                z_ref, xbc_ref, sc_ref, gt_ref, dt_ref, q_ref, k_ref, v_ref, uv_ref,
                qn0_ref, kvn0_ref, fn0_ref, sm0_ref, qn1_ref, kvn1_ref, fn1_ref, sm1_ref):
    j = pl.program_id(1)

    def stage_a(qn_ref, kvn_ref, fn_ref, sm_ref):
        mod = mod_ref[0]
        hb = (_rms(x_ref[0], g1_ref[...]) * (1.0 + mod[1:2]) + mod[0:1]).astype(BF16)

        def proj(lo, hi):
            return _dot(hb, w_ref[:, lo:hi])

        qn_ref[...] = _rms(proj(_C_QA, _C_KVA), qag_ref[...]).astype(BF16)
        kvn_ref[...] = _rms(proj(_C_KVA, _C_FN), kvag_ref[...]).astype(BF16)
        fn_ref[...] = proj(_C_FN, _C_GT).astype(BF16)
        small = proj(_C_SM, _C_END)
        sm_ref[...] = small
        dt_ref[0] = small
        z_ref[0] = proj(_C_Z, _C_XBC).astype(BF16)
        xbc_ref[0] = proj(_C_XBC, _C_SC).astype(BF16)
        sc_ref[0] = proj(_C_SC, _C_QA).astype(BF16)
        gt_ref[0] = (1.0 / (1.0 + jnp.exp2(proj(_C_GT, _C_SM)))).astype(BF16)

    def stage_b(qn_ref, kvn_ref, fn_ref, sm_ref):
        tab = tab_ref[...]
        q = _dot(qn_ref[...], wqb_ref[...])
        for hd in range(MLA_HEADS):
            sl = slice(hd * LANES, (hd + 1) * LANES)
            q_ref[0, :, sl] = _rope(q[:, sl], tab).astype(BF16)
        kvn = kvn_ref[...]
        kn = _dot(kvn, wkk_ref[...])
        small = sm_ref[...]
        lane = lax.broadcasted_iota(jnp.int32, small.shape, 1)
        kr = _rope(jnp.where(lane >= _KR_LANE, small, 0.0), tab)
        for hd in range(MLA_HEADS):
            sl = slice(hd * LANES, (hd + 1) * LANES)
            k_ref[0, :, sl] = (kn[:, sl] + kr).astype(BF16)
        v_ref[0] = _dot(kvn, wkv_ref[...]).astype(BF16)
        uv_ref[0] = _dot(fn_ref[...], wc_ref[...]).astype(BF16)

    bufs = ((qn0_ref, kvn0_ref, fn0_ref, sm0_ref), (qn1_ref, kvn1_ref, fn1_ref, sm1_ref))
    variants = {}
    for step in range(nt + 1):
        variants.setdefault((step < nt, step >= 1, step % 2), []).append(step)
    for (do_a, do_b, parity), steps in variants.items():
        cond = functools.reduce(jnp.logical_or, [j == st for st in steps])

        def branch(do_a=do_a, do_b=do_b, parity=parity):
            if do_b:
                stage_b(*bufs[1 - parity])
            if do_a:
                stage_a(*bufs[parity])

        pl.when(cond)(branch)


def _pre(x, mod, g1, w1, qag, wqb, kvag, wkk, wkv, wc, tab, nct):
    b, t, d = x.shape
    nt = t // ROW_TILE
    nb = mod.shape[0] - 1

    def ta(j):
        return jnp.minimum(j, nt - 1)

    def tb(j):
        return jnp.maximum(j - 1, 0)

    def row(width, tile, dtype=BF16):
        return (pl.BlockSpec((1, ROW_TILE, width), lambda bi, j: (bi, tile(j), 0)),
                jax.ShapeDtypeStruct((b, t, width), dtype))

    outs = [row(512, ta), row(1024, ta), row(1536, ta), row(4096, ta), row(LANES, ta, F32),
            row(1024, tb), row(1024, tb), row(512, tb), row(1024, tb)]
    stage_bufs = [pltpu.VMEM((ROW_TILE, _C_KVA - _C_QA), BF16), pltpu.VMEM((ROW_TILE, _C_FN - _C_KVA), BF16),
                  pltpu.VMEM((ROW_TILE, _C_GT - _C_FN), BF16), pltpu.VMEM((ROW_TILE, LANES), F32)]
    return pl.pallas_call(
        functools.partial(_pre_kernel, nt),
        grid=(b, nt + 1),
        in_specs=[
            pl.BlockSpec((1, ROW_TILE, d), lambda bi, j: (bi, ta(j), 0)),
            pl.BlockSpec((1, 6, d), lambda bi, j: (jnp.where(ta(j) < nct, nb, bi), 0, 0)),
            _const_spec(g1.shape), _const_spec(w1.shape), _const_spec(qag.shape),
            _const_spec(wqb.shape), _const_spec(kvag.shape), _const_spec(wkk.shape),
            _const_spec(wkv.shape), _const_spec(wc.shape),
            pl.BlockSpec((ROW_TILE, 3 * LANES), lambda bi, j: (tb(j), 0)),
        ],
        out_specs=[o[0] for o in outs],
        out_shape=[o[1] for o in outs],
        scratch_shapes=stage_bufs + stage_bufs,
        compiler_params=_params("arbitrary", "arbitrary"),
        name="pre_proj",
    )(x, mod, g1, w1, qag, wqb, kvag, wkk, wkv, wc, tab)


def _halo_conv(top, mid, bot, top_ok, bot_ok, w, ksize):
    rows = mid.shape[0]
    full = jnp.concatenate([jnp.where(top_ok, top, 0.0), mid, jnp.where(bot_ok, bot, 0.0)], axis=0)
    n = full.shape[0]
    pad = (ksize - 1) // 2
    acc = None
    for k in range(ksize):
        d = k - pad
        sh = full if d == 0 else pltpu.roll(full, (-d) % n, 0)
        term = sh[HALO:HALO + rows] * w[k:k + 1]
        acc = term if acc is None else acc + term
    return acc


def _split3(v, axis):
    hi = v.astype(BF16)
    r1 = v - hi.astype(F32)
    mid = r1.astype(BF16)
    lo = (r1 - mid.astype(F32)).astype(BF16)
    return jnp.concatenate([hi, mid, lo], axis=axis)


def _ssd_chunk(xs, bm, cm, dts, a_row, tri3, st_ref, direction):
    q = xs.shape[0]
    off = SSM_HEADS * direction
    la = dts * a_row
    cum = _dot(tri3, _split3(la, 0))
    cum_t = cum.T[off:off + SSM_HEADS]
    lane = lax.broadcasted_iota(jnp.int32, (q, LANES), 1)
    first_half = lane < SSM_HEAD_DIM

    def spread(v):
        return [jnp.broadcast_to(v[:, off + hd:off + hd + 1], (q, LANES)) for hd in range(SSM_HEADS)]

    def pairs(blocks):
        return jnp.concatenate([jnp.where(first_half, blocks[2 * p], blocks[2 * p + 1])
                                for p in range(SSM_HEADS // 2)], axis=1)

    cum_blk = spread(cum)
    cum_w = jnp.concatenate(cum_blk, axis=1)
    cum_x = pairs(cum_blk)
    dt_x = pairs(spread(dts))
    last = q - 1 if direction == 0 else 0
    tot_x = cum_x[last:last + 1]
    xdt = xs * dt_x
    xw = (xdt * jnp.exp2(tot_x - cum_x)).astype(BF16)
    xdt_b = xdt.astype(BF16)
    keep = tri3[:, 0:q] > 0
    y_pairs = []
    y_off = []
    for g in range(SSM_GROUPS):
        cg = cm[:, g * SSM_STATE:(g + 1) * SSM_STATE]
        bg = bm[:, g * SSM_STATE:(g + 1) * SSM_STATE]
        cb = _dot_nt(cg, bg)
        yd = []
        for hl in range(SSM_HPG):
            hd = g * SSM_HPG + hl
            ci = cum_w[:, hd * LANES:(hd + 1) * LANES]
            cj = cum_t[hd:hd + 1, :]
            seg = jnp.exp2(jnp.where(keep, ci - cj, -jnp.inf))
            m = (cb * seg).astype(BF16)
            pair = hd // 2
            yd.append(_dot(m, xdt_b[:, pair * LANES:(pair + 1) * LANES]))
        for pr in range(SSM_HPG // 2):
            y_pairs.append(jnp.where(first_half, yd[2 * pr], yd[2 * pr + 1]))
        cols = slice(g * SSM_HPG * SSM_HEAD_DIM, (g + 1) * SSM_HPG * SSM_HEAD_DIM)
        st = st_ref[:, cols]
        y_off.append(_dot(cg, st.astype(BF16)))
        upd = _dot(bg.astype(F32).T.astype(BF16), xw[:, cols])
        st_ref[:, cols] = st * jnp.exp2(tot_x[:, cols]) + upd
    y_diag = jnp.concatenate(y_pairs, axis=1)
    return y_diag + jnp.concatenate(y_off, axis=1) * jnp.exp2(cum_x)


def _ssd_f_kernel(nct, nt, xbc_ref, top_ref, bot_ref, dt_ref, cw_ref, cb_ref, alog_ref, dtb_ref,
                  tri_ref, xact_ref, yf_ref, st_ref):
    j = pl.program_id(1)

    @pl.when(j == 0)
    def _():
        st_ref[...] = jnp.zeros_like(st_ref)

    top_ok = jnp.logical_and(j != 0, j != nct)
    bot_ok = jnp.logical_and(j != nct - 1, j != nt - 1)
    conv = _halo_conv(top_ref[0].astype(F32), xbc_ref[0].astype(F32), bot_ref[0].astype(F32),
                      top_ok, bot_ok, cw_ref[...], SSM_CONV)
    act = _silu(conv + cb_ref[...])
    actb = act.astype(BF16)
    xact_ref[0] = actb
    dts = _softplus(dt_ref[0] + dtb_ref[...])
    a_row = -jnp.exp(alog_ref[...]) * math.log2(math.e)
    d_in = SSM_HEADS * SSM_HEAD_DIM
    for c in range(ROW_TILE // SSM_CHUNK):
        r = slice(c * SSM_CHUNK, (c + 1) * SSM_CHUNK)
        yf_ref[0, r, :] = _ssd_chunk(act[r, 0:d_in], actb[r, d_in:d_in + 256], actb[r, d_in + 256:],
                                     dts[r], a_row, tri_ref[0], st_ref, 0)


def _ssd_b_kernel(xact_ref, dt_ref, z_ref, yf_ref, alog_ref, dtb_ref, dsk_ref, ng_ref,
                  tri_ref, y_ref, st_ref):
    j = pl.program_id(1)

    @pl.when(j == 0)
    def _():
        st_ref[...] = jnp.zeros_like(st_ref)

    actb = xact_ref[0]
    dts = _softplus(dt_ref[0] + dtb_ref[...])
    a_row = -jnp.exp(alog_ref[...]) * math.log2(math.e)
    d_in = SSM_HEADS * SSM_HEAD_DIM
    half = d_in // SSM_GROUPS
    for c in reversed(range(ROW_TILE // SSM_CHUNK)):
        r = slice(c * SSM_CHUNK, (c + 1) * SSM_CHUNK)
        xs = actb[r, 0:d_in].astype(F32)
        yb = _ssd_chunk(xs, actb[r, d_in:d_in + 256], actb[r, d_in + 256:], dts[r], a_row,
                        tri_ref[1], st_ref, 1)
        y = (yf_ref[0, r, :] + yb + dsk_ref[...] * xs) * _silu(z_ref[0, r, :].astype(F32))
        for g in range(SSM_GROUPS):
            cols = slice(g * half, (g + 1) * half)
            y_ref[0, r, cols] = _rms(y[:, cols], ng_ref[:, cols]).astype(BF16)


def _ssd(xbc, dt, z, cw, cb, alog, dtb, dsk, ng, tri3, nct):
    b, t, c = xbc.shape
    nt = t // ROW_TILE
    hb = ROW_TILE // HALO
    nh = t // HALO
    d_in = SSM_HEADS * SSM_HEAD_DIM
    tile = lambda width: pl.BlockSpec((1, ROW_TILE, width), lambda bi, j: (bi, j, 0))
    xact, yf = pl.pallas_call(
        functools.partial(_ssd_f_kernel, nct, nt),
        grid=(b, nt),
        in_specs=[
            tile(c),
            pl.BlockSpec((1, HALO, c), lambda bi, j: (bi, jnp.maximum(j * hb - 1, 0), 0)),
            pl.BlockSpec((1, HALO, c), lambda bi, j: (bi, jnp.minimum((j + 1) * hb, nh - 1), 0)),
            tile(LANES),
            _const_spec(cw.shape), _const_spec(cb.shape), _const_spec(alog.shape),
            _const_spec(dtb.shape), _const_spec(tri3.shape),
        ],
        out_specs=[tile(c), tile(d_in)],
        out_shape=[jax.ShapeDtypeStruct((b, t, c), BF16), jax.ShapeDtypeStruct((b, t, d_in), F32)],
        scratch_shapes=[pltpu.VMEM((SSM_STATE, d_in), F32)],
        compiler_params=_params("parallel", "arbitrary"),
        name="ssd_fwd",
    )(xbc, xbc, xbc, dt, cw, cb, alog, dtb, tri3)

    def rev(j):
        return jnp.where(j < nct, nct - 1 - j, nt - 1 - (j - nct))

    rtile = lambda width: pl.BlockSpec((1, ROW_TILE, width), lambda bi, j: (bi, rev(j), 0))
    return pl.pallas_call(
        _ssd_b_kernel,
        grid=(b, nt),
        in_specs=[
            rtile(c), rtile(LANES), rtile(d_in), rtile(d_in),
            _const_spec(alog.shape), _const_spec(dtb.shape), _const_spec(dsk.shape),
            _const_spec(ng.shape), _const_spec(tri3.shape),
        ],
        out_specs=rtile(d_in),
        out_shape=jax.ShapeDtypeStruct((b, t, d_in), BF16),
        scratch_shapes=[pltpu.VMEM((SSM_STATE, d_in), F32)],
        compiler_params=_params("parallel", "arbitrary"),
        name="ssd_bwd",
    )(xact, dt, z, yf, alog, dtb, dsk, ng, tri3)


def _attn_kernel(nct, nsub, q_ref, k_ref, v_ref, o_ref, vp_ref):
    j = pl.program_id(2)
    nkeys = k_ref.shape[1]
    nsteps = nkeys // ROW_TILE // nsub
    first = lax.broadcasted_iota(jnp.int32, (ROW_TILE, LANES), 1) < MLA_V
    one_lane = (MLA_V, 0)

    @pl.when(j == 0)
    def _():
        v = v_ref[0]
        lane = lax.broadcasted_iota(jnp.int32, v.shape, 1)
        for hh in range(2):
            own = (lane < MLA_V) if hh == 0 else (lane >= MLA_V)
            fill = jnp.where(lane == one_lane[hh], 1.0, 0.0).astype(BF16)
            vp_ref[hh] = jnp.where(own, v, fill)

    def attend(sub, nk, kmax):
        rs = slice(sub * ROW_TILE, (sub + 1) * ROW_TILE)
        outs, flags = [], []
        for hh in range(2):
            sl = slice(hh * LANES, (hh + 1) * LANES)
            q = q_ref[0, rs, sl]
            s = _dot_nt(q, k_ref[0, 0:nk, sl])
            if kmax is None:
                shift = jnp.max(s, axis=-1, keepdims=True)
            else:
                qf = q.astype(F32)
                shift = jnp.sqrt(jnp.sum(qf * qf, axis=-1, keepdims=True)) * kmax[nk][hh] - ATTN_SHIFT_MARGIN
            r = _dot(jnp.exp2(s - shift).astype(BF16), vp_ref[hh, 0:nk, :])
            l = r[:, one_lane[hh]:one_lane[hh] + 1]
            outs.append(r / l)
            if kmax is not None:
                ok = jnp.logical_and(l >= 2.0 ** -ATTN_SHIFT_MARGIN, l <= 2.0 ** (2 * ATTN_SHIFT_MARGIN))
                flags.append(jnp.where(ok, 0.0, 1.0))
        o_ref[0, rs, :] = jnp.where(first, outs[0], outs[1]).astype(BF16)
        return flags

    def keys_of(tile):
        return nct * ROW_TILE if tile < nct else nkeys

    variants = {}
    for step in range(nsteps):
        variants.setdefault(tuple(keys_of(step * nsub + sub) for sub in range(nsub)), []).append(step)
    for keys, steps in variants.items():
        cond = functools.reduce(jnp.logical_or, [j == st for st in steps])

        def branch(keys=keys):
            kmax = {}
            for nk in set(keys):
                kf = k_ref[0, 0:nk, :].astype(F32)
                sq = kf * kf
                kmax[nk] = [jnp.sqrt(jnp.max(jnp.sum(sq[:, hh * LANES:(hh + 1) * LANES], axis=-1,
                                                     keepdims=True), axis=0, keepdims=True))
                            for hh in range(2)]
            flags = []
            for sub, nk in enumerate(keys):
                flags += attend(sub, nk, kmax)
            unsafe = functools.reduce(jnp.maximum, [jnp.max(f) for f in flags]) > 0.0

            @pl.when(unsafe)
            def _():
                for sub, nk in enumerate(keys):
                    attend(sub, nk, None)

        pl.when(cond)(branch)


def _attention(q, k, v, nct):
    b, t, _ = q.shape
    nt = t // ROW_TILE
    pairs = MLA_HEADS // 2
    nsub = next(n for n in ATTN_TILES_PER_STEP if nt % n == 0)
    rows = nsub * ROW_TILE
    return pl.pallas_call(
        functools.partial(_attn_kernel, nct, nsub),
        grid=(b, pairs, nt // nsub),
        in_specs=[
            pl.BlockSpec((1, rows, 2 * LANES), lambda bi, p, j: (bi, j, p)),
            pl.BlockSpec((1, t, 2 * LANES), lambda bi, p, j: (bi, 0, p)),
            pl.BlockSpec((1, t, LANES), lambda bi, p, j: (bi, 0, p)),
        ],
        out_specs=pl.BlockSpec((1, rows, LANES), lambda bi, p, j: (bi, j, p)),
        out_shape=jax.ShapeDtypeStruct((b, t, pairs * LANES), BF16),
        scratch_shapes=[pltpu.VMEM((2, t, LANES), BF16)],
        compiler_params=_params("parallel", "arbitrary", "arbitrary"),
        name="mla_attn",
    )(q, k, v)


def _fmix_kernel(nct, uv_ref, cl_ref, sl_ref, cc_ref, sc_ref, o_ref):
    half = uv_ref.shape[2] // 2
    nctx = nct * ROW_TILE
    nlat = uv_ref.shape[1] - nctx
    for r in range(0, nctx, ROW_TILE):
        o_ref[0, r:r + ROW_TILE, :] = (
            _dot(cc_ref[r:r + ROW_TILE, :], uv_ref[0, 0:nctx, 0:half])
            + _dot(sc_ref[r:r + ROW_TILE, :], uv_ref[0, 0:nctx, half:])).astype(BF16)
    for r in range(0, nlat, ROW_TILE):
        o_ref[0, nctx + r:nctx + r + ROW_TILE, :] = (
            _dot(cl_ref[r:r + ROW_TILE, :], uv_ref[0, nctx:, 0:half])
            + _dot(sl_ref[r:r + ROW_TILE, :], uv_ref[0, nctx:, half:])).astype(BF16)


def _fmix(uv, cl, sl, cc, sc, nct):
    b, t, w = uv.shape
    return pl.pallas_call(
        functools.partial(_fmix_kernel, nct),
        grid=(b,),
        in_specs=[
            pl.BlockSpec((1, t, w), lambda bi: (bi, 0, 0)),
            _const_spec(cl.shape), _const_spec(sl.shape), _const_spec(cc.shape), _const_spec(sc.shape),
        ],
        out_specs=pl.BlockSpec((1, t, w // 2), lambda bi: (bi, 0, 0)),
        out_shape=jax.ShapeDtypeStruct((b, t, w // 2), BF16),
        compiler_params=_params("parallel"),
        name="fourier_mix",
    )(uv, cl, sl, cc, sc)


def _post_kernel(nct, nt, last, x_ref, mod_ref, ys_ref, sc_ref, top_ref, bot_ref, at_ref, fo_ref,
                 gt_ref, scw_ref, wb_ref, wo_ref, g2_ref, w1_ref, w2_ref, fg_ref, o_ref):
    j = pl.program_id(1)

    def body():
        mod = mod_ref[0]
        w = sc_ref.shape[2] // 3
        top_ok = jnp.logical_and(j != 0, j != nct)
        bot_ok = jnp.logical_and(j != nct - 1, j != nt - 1)

        def cx(ref):
            return ref[0, :, w:2 * w].astype(F32) * ref[0, :, 2 * w:3 * w].astype(F32)

        conv = _halo_conv(cx(top_ref), cx(sc_ref), cx(bot_ref), top_ok, bot_ok, scw_ref[...], SC_CONV)
        branches = [ys_ref[0], (sc_ref[0, :, 0:w].astype(F32) * conv).astype(BF16), at_ref[0],
                    fo_ref[0]]
        d = x_ref.shape[2]
        merged = None
        for k in range(N_BRANCHES):
            term = gt_ref[0, :, k * d:(k + 1) * d].astype(F32) * _dot(branches[k], wb_ref[k])
            merged = term if merged is None else merged + term
        x1 = x_ref[0] + mod[2:3] * _dot(merged.astype(BF16), wo_ref[...])
        h2 = (_rms(x1, g2_ref[...]) * (1.0 + mod[4:5]) + mod[3:4]).astype(BF16)
        hid = jnp.maximum(_dot(h2, w1_ref[...]), 0.0)
        x2 = x1 + mod[5:6] * _dot((hid * hid).astype(BF16), w2_ref[...])
        o_ref[0] = _rms(x2, fg_ref[...]) if last else x2

    if last:
        pl.when(j >= nct)(body)
    else:
        body()


def _post(x, mod, ys, sc, at, fo, gt, scw, wb, wo, g2, w1, w2, fg, nct, last):
    b, t, d = x.shape
    nt = t // ROW_TILE
    nb = mod.shape[0] - 1
    hb = ROW_TILE // HALO
    nh = t // HALO
    tile = lambda width: pl.BlockSpec((1, ROW_TILE, width), lambda bi, j: (bi, j, 0))
    scw_w = sc.shape[2]
    if last:
        out_spec = pl.BlockSpec((1, ROW_TILE, d), lambda bi, j: (bi, jnp.maximum(j - nct, 0), 0))
        out_shape = jax.ShapeDtypeStruct((b, t - nct * ROW_TILE, d), F32)
        alias = {}
    else:
        out_spec, out_shape, alias = tile(d), jax.ShapeDtypeStruct((b, t, d), F32), {0: 0}
    return pl.pallas_call(
        functools.partial(_post_kernel, nct, nt, last),
        grid=(b, nt),
        in_specs=[
            tile(d),
            pl.BlockSpec((1, 6, d), lambda bi, j: (jnp.where(j < nct, nb, bi), 0, 0)),
            tile(ys.shape[2]), tile(scw_w),
            pl.BlockSpec((1, HALO, scw_w), lambda bi, j: (bi, jnp.maximum(j * hb - 1, 0), 0)),
            pl.BlockSpec((1, HALO, scw_w), lambda bi, j: (bi, jnp.minimum((j + 1) * hb, nh - 1), 0)),
            tile(at.shape[2]), tile(fo.shape[2]), tile(gt.shape[2]),
            _const_spec(scw.shape), _const_spec(wb.shape), _const_spec(wo.shape),
            _const_spec(g2.shape), _const_spec(w1.shape), _const_spec(w2.shape),
            _const_spec(fg.shape),
        ],
        out_specs=out_spec,
        out_shape=out_shape,
        input_output_aliases=alias,
        compiler_params=_params("parallel", "arbitrary"),
        name="post_merge_mlp",
    )(x, mod, ys, sc, sc, sc, at, fo, gt, scw, wb, wo, g2, w1, w2, fg)


def _rope_table(seq, ctx):
    nf = MLA_ROPE // 4
    pos = jnp.arange(seq)
    inv = ROPE_THETA ** (-jnp.arange(nf, dtype=F32) / nf)
    ang = jnp.stack([(pos // GRID_W).astype(F32)[:, None] * inv,
                     (pos % GRID_W).astype(F32)[:, None] * inv], axis=1)
    cos, sin = jnp.cos(ang), jnp.sin(ang)
    zero = jnp.zeros_like(sin)
    cos_l = jnp.stack([cos, cos], axis=2).reshape(seq, MLA_ROPE)
    sa_l = jnp.stack([-sin, zero], axis=2).reshape(seq, MLA_ROPE)
    sb_l = jnp.stack([zero, sin], axis=2).reshape(seq, MLA_ROPE)

    def lanes(v, fill):
        full = jnp.full((ctx + seq, LANES), fill, F32)
        return full.at[ctx:, _KR_LANE:_KR_LANE + MLA_ROPE].set(v)

    return jnp.concatenate([lanes(cos_l, 1.0), lanes(sa_l, 0.0), lanes(sb_l, 0.0)], axis=1)


def _dft_mats(n):
    w = math.gcd(n, 64)
    k = jnp.arange(n, dtype=jnp.int32)[:, None]

    def table(cols):
        ang = ((k * cols[None, :]) % n).astype(F32) * (2.0 * math.pi / n)
        return jnp.cos(ang), jnp.sin(ang)

    ca, sa = table(jnp.arange(n // w, dtype=jnp.int32) * w)
    cb, sb = table(jnp.arange(w, dtype=jnp.int32))
    s = n ** -0.5
    cos = ca[:, :, None] * cb[:, None, :] - sa[:, :, None] * sb[:, None, :]
    sin = sa[:, :, None] * cb[:, None, :] + ca[:, :, None] * sb[:, None, :]
    return (cos * s).reshape(n, n).astype(BF16), (-sin * s).reshape(n, n).astype(BF16)


def _channel_dft(width):
    gd = width // FNET_GROUPS
    idx = np.arange(gd)
    ang = 2.0 * np.pi * ((idx[:, None] * idx[None, :]) % gd) / gd
    eye = np.eye(FNET_GROUPS)
    c = np.kron(eye, np.cos(ang)) / math.sqrt(gd)
    s = np.kron(eye, np.sin(ang)) / math.sqrt(gd)
    return jnp.asarray(np.concatenate([c, s], axis=1), BF16)


def _head_pad(w, width, used):
    k = w.shape[0]
    w = w.reshape(k, MLA_HEADS, width)[:, :, :used]
    return jnp.pad(w, ((0, 0), (0, 0), (0, LANES - used))).reshape(k, MLA_HEADS * LANES)


def kernel(x, c, ctx, c_ctx, ada_w, ada_b, norm1_g, norm2_g, w_in, ssm_conv_w, ssm_conv_b, ssm_a_log, ssm_dt_bias, ssm_d, ssm_norm_g, sc_conv_w, mla_qa_g, mla_wqb, mla_kva_g, mla_wkvb, w_branch, w_out, mlp_w1, mlp_w2, final_norm_g):
    b, seq, d = x.shape
    nctx = ctx.shape[1]
    depth = ada_w.shape[0]
    assert seq % ROW_TILE == 0 and nctx % ROW_TILE == 0 and seq % GRID_W == 0
    nct = nctx // ROW_TILE
    d_in = SSM_HEADS * SSM_HEAD_DIM

    xcat = jnp.concatenate([ctx, x], axis=1)
    mods = _modulation(jnp.concatenate([c, c_ctx[None]], axis=0), ada_w, ada_b)

    tab = _rope_table(seq, nctx)
    cl, sl = _dft_mats(seq)
    cc, sc_m = _dft_mats(nctx)
    wc = _channel_dft(d // 2)
    tri_np = np.tril(np.ones((SSM_CHUNK, SSM_CHUNK), np.float32))
    tri3 = jnp.asarray(np.stack([np.tile(tri_np, (1, 3)), np.tile(tri_np.T, (1, 3))]), BF16)

    def lane_pad(v):
        return jnp.pad(v.reshape(1, -1), ((0, 0), (0, LANES - v.size)))

    for i in range(depth):
        wz, wxbc, wdt, wsc, wqa, wkva, wkr, wfn, wgt = jnp.split(
            w_in[i], np.cumsum([512, 1024, 16, 1536, 384, 256, 32, 512])[:].tolist(), axis=1)
        small = jnp.zeros((d, LANES), F32)
        small = small.at[:, _DT_LANE:_DT_LANE + 16].set(wdt).at[:, _KR_LANE:_KR_LANE + MLA_ROPE].set(wkr)
        w1 = jnp.concatenate([wz, wxbc, wsc, wqa, wkva, wfn, wgt * -math.log2(math.e), small],
                             axis=1).astype(BF16)
        q_scale = (MLA_NOPE + MLA_ROPE) ** -0.5 * math.log2(math.e)
        wqb = _head_pad(mla_wqb[i] * q_scale, MLA_NOPE + MLA_ROPE, MLA_NOPE + MLA_ROPE).astype(BF16)
        wkk = _head_pad(mla_wkvb[i], MLA_NOPE + MLA_V, MLA_NOPE).astype(BF16)
        wkv = mla_wkvb[i].reshape(-1, MLA_HEADS, MLA_NOPE + MLA_V)[:, :, MLA_NOPE:]
        wkv = wkv.reshape(-1, MLA_HEADS * MLA_V).astype(BF16)

        z, xbc, sc, gt, dt, q, k, v, uv = _pre(
            xcat, mods[i], norm1_g[i][None], w1, mla_qa_g[i][None], wqb, mla_kva_g[i][None], wkk, wkv,
            wc, tab, nct)
        ys = _ssd(xbc, dt, z, ssm_conv_w[i], ssm_conv_b[i][None], lane_pad(ssm_a_log[i]),
                  lane_pad(ssm_dt_bias[i]), jnp.repeat(ssm_d[i], SSM_HEAD_DIM)[None],
                  ssm_norm_g[i][None], tri3, nct)
        at = _attention(q, k, v, nct)
        fo = _fmix(uv, cl, sl, cc, sc_m, nct)
        xcat = _post(xcat, mods[i], ys, sc, at, fo, gt, sc_conv_w[i], w_branch[i].astype(BF16),
                     w_out[i].astype(BF16), norm2_g[i][None], mlp_w1[i].astype(BF16),
                     mlp_w2[i].astype(BF16), final_norm_g[None], nct, i == depth - 1)
    return xcat
```
